```python
import math
import jax, jax.numpy as jnp
from jax import lax
import numpy as np

D_MODEL = 2048
BATCH = 1
SEQ = 8192
DEPTH = 2

N_MIXERS = 2
N_DN_LAYERS = (DEPTH + 1) // 2
N_FOX_LAYERS = DEPTH // 2
DN_QK_HEADS = D_MODEL // 128
DN_V_HEADS = D_MODEL // 64
DN_HEAD_DIM = 128
DN_QK_DIM = DN_QK_HEADS * DN_HEAD_DIM
DN_V_DIM = DN_V_HEADS * DN_HEAD_DIM
DN_CONV_DIM = 2 * DN_QK_DIM + DN_V_DIM
DN_IN_DIM = DN_CONV_DIM + DN_V_DIM + 2 * DN_V_HEADS
DN_CONV_WIDTH = 4
DN_CHUNK = 64
FOX_HEADS = D_MODEL // 128
FOX_HEAD_DIM = 128
FOX_DIM = FOX_HEADS * FOX_HEAD_DIM
FOX_IN_DIM = 4 * FOX_DIM + FOX_HEADS
FOX_Q_BLOCK = 128
MOE_GROUPS = 4
MOE_EXPERTS_PER_GROUP = 8
MOE_TOP_K = 2
MOE_D_EXPERT = D_MODEL // 4
DEEPNORM_ALPHA = (2 * DEPTH) ** 0.25
DEEPNORM_BETA = (8 * DEPTH) ** -0.25
NORM_EPS = 1e-6

kernel_name = 'hybrid_gdn_fox_hmoe_deepnorm'

F32 = jnp.float32


def _layernorm(x, g, b):
    xf = x.astype(F32)
    mu = jnp.mean(xf, axis=-1, keepdims=True)
    xc = xf - mu
    var = jnp.mean(xc * xc, axis=-1, keepdims=True)
    return (xc * lax.rsqrt(var + NORM_EPS) * g.astype(F32) + b.astype(F32)).astype(x.dtype)


def _rmsnorm_f32(x, w):
    xf = x.astype(F32)
    return xf * lax.rsqrt(jnp.mean(xf * xf, axis=-1, keepdims=True) + NORM_EPS) * w.astype(F32)


def _l2norm(x):
    return x * lax.rsqrt(jnp.sum(x * x, axis=-1, keepdims=True) + NORM_EPS)


def _causal_dwconv(u, w):
    k_width, c = w.shape
    return lax.conv_general_dilated(u, w[:, None, :].astype(u.dtype), window_strides=(1,),
                                    padding=[(k_width - 1, 0)],
                                    dimension_numbers=('NWC', 'WIO', 'NWC'),
                                    feature_group_count=c)


def _chunk_gated_delta_rule(q, k, v, g, beta):
    b_, s_, h_, dk = q.shape
    dv = v.shape[-1]
    c_ = DN_CHUNK
    n_ = s_ // c_

    def to_chunks(t):
        return t.reshape(b_, n_, c_, h_, -1).transpose(0, 3, 1, 2, 4)

    q, k, v = to_chunks(q), to_chunks(k), to_chunks(v)
    g = g.reshape(b_, n_, c_, h_).transpose(0, 3, 1, 2)
    beta = beta.reshape(b_, n_, c_, h_).transpose(0, 3, 1, 2)
    G = jnp.cumsum(g, axis=-1)
    causal = jnp.tril(jnp.ones((c_, c_), dtype=bool))
    strict = jnp.tril(jnp.ones((c_, c_), dtype=bool), -1)
    diff = G[..., :, None] - G[..., None, :]
    decay = jnp.where(causal, jnp.exp(jnp.where(causal, diff, 0.0)), 0.0)
    kb = k * beta[..., None]
    lower = jnp.where(strict, jnp.einsum('bhnid,bhnjd->bhnij', kb, k) * decay, 0.0)
    a_mat = lower + jnp.eye(c_, dtype=F32)
    rhs = jnp.concatenate([v * beta[..., None], kb * jnp.exp(G)[..., None]], axis=-1)
    sol = lax.linalg.triangular_solve(a_mat, rhs, left_side=True, lower=True)
    u, w = sol[..., :dv], sol[..., dv:]
    attn = jnp.einsum('bhnid,bhnjd->bhnij', q, k) * decay
    g_last = G[..., -1]
    k_dec = k * jnp.exp(g_last[..., None] - G)[..., None]
    q_dec = q * jnp.exp(G)[..., None]

    def step(state, xs):
        q_n, w_n, u_n, attn_n, k_n, gl_n = xs
        v_new = u_n - jnp.einsum('bhcd,bhde->bhce', w_n, state)
        o_n = jnp.einsum('bhcd,bhde->bhce', q_n, state) + jnp.einsum('bhij,bhje->bhie', attn_n, v_new)
        state = state * jnp.exp(gl_n)[..., None, None] + jnp.einsum('bhcd,bhce->bhde', k_n, v_new)
        return state, o_n

    xs = tuple(jnp.moveaxis(t, 2, 0) for t in (q_dec, w, u, attn, k_dec, g_last))
    state0 = jnp.zeros((b_, h_, dk, dv), F32)
    _, o = lax.scan(step, state0, xs)
    return o.transpose(1, 0, 3, 2, 4).reshape(b_, s_, h_, dv)


def gated_deltanet(x, w_in, conv_w, a_log, dt_bias, norm_w, w_out):
    b_, s_, _ = x.shape
    proj = x @ w_in
    qkv, z, b_logit, a_logit = jnp.split(proj, [DN_CONV_DIM, DN_CONV_DIM + DN_V_DIM,
                                                 DN_CONV_DIM + DN_V_DIM + DN_V_HEADS], axis=-1)
    qkv = jax.nn.silu(_causal_dwconv(qkv, conv_w)).astype(F32)
    q, k, v = jnp.split(qkv, [DN_QK_DIM, 2 * DN_QK_DIM], axis=-1)
    rep = DN_V_HEADS // DN_QK_HEADS
    q = jnp.repeat(_l2norm(q.reshape(b_, s_, DN_QK_HEADS, DN_HEAD_DIM)) * DN_HEAD_DIM ** -0.5, rep, axis=2)
    k = jnp.repeat(_l2norm(k.reshape(b_, s_, DN_QK_HEADS, DN_HEAD_DIM)), rep, axis=2)
    v = v.reshape(b_, s_, DN_V_HEADS, DN_HEAD_DIM)
    beta = jax.nn.sigmoid(b_logit.astype(F32))
    g = -jnp.exp(a_log.astype(F32)) * jax.nn.softplus(a_logit.astype(F32) + dt_bias.astype(F32))
    o = _chunk_gated_delta_rule(q, k, v, g, beta)
    zf = z.reshape(b_, s_, DN_V_HEADS, DN_HEAD_DIM).astype(F32)
    o = _rmsnorm_f32(o, norm_w) * jax.nn.silu(zf)
    return o.reshape(b_, s_, DN_V_DIM).astype(x.dtype) @ w_out


def forgetting_attention(x, w_in, b_f, q_norm_w, k_norm_w, w_out):
    b_, s_, _ = x.shape
    h_, hd = FOX_HEADS, FOX_HEAD_DIM
    proj = x @ w_in
    q, k, v, gate, f_logit = jnp.split(proj, [FOX_DIM, 2 * FOX_DIM, 3 * FOX_DIM, 4 * FOX_DIM], axis=-1)
    q = _rmsnorm_f32(q.reshape(b_, s_, h_, hd), q_norm_w).astype(x.dtype).transpose(0, 2, 1, 3)
    k = _rmsnorm_f32(k.reshape(b_, s_, h_, hd), k_norm_w).astype(x.dtype).transpose(0, 2, 1, 3)
    v = v.reshape(b_, s_, h_, hd).transpose(0, 2, 1, 3)
    log_f = jax.nn.log_sigmoid(f_logit.astype(F32) + b_f.astype(F32))
    c = jnp.cumsum(log_f, axis=1).transpose(0, 2, 1)
    nb = s_ // FOX_Q_BLOCK
    q_blocks = q.reshape(b_, h_, nb, FOX_Q_BLOCK, hd).transpose(2, 0, 1, 3, 4)
    c_blocks = c.reshape(b_, h_, nb, FOX_Q_BLOCK).transpose(2, 0, 1, 3)
    pos = jnp.arange(s_, dtype=jnp.int32)
    q_pos = pos.reshape(nb, FOX_Q_BLOCK)
    scale = hd ** -0.5

    def attend_block(args):
        qb, cb, pb = args
        s = jnp.einsum('bhqd,bhkd->bhqk', qb, k, preferred_element_type=F32) * scale
        s = s + cb[..., :, None] - c[..., None, :]
        s = jnp.where(pb[:, None] >= pos[None, :], s, -jnp.inf)
        p = jax.nn.softmax(s, axis=-1)
        return jnp.einsum('bhqk,bhkd->bhqd', p.astype(v.dtype), v)

    o = lax.map(attend_block, (q_blocks, c_blocks, q_pos))
    o = o.transpose(1, 0, 3, 2, 4).reshape(b_, s_, FOX_DIM)
    o = o * jax.nn.sigmoid(gate)
    return o @ w_out


def hierarchical_moe(x, w_rg, b_rg, w_re, b_re, w_gate, w_up, w_down):
    lg = (jnp.einsum('bsd,dg->bsg', x, w_rg) + b_rg).astype(F32)
    pg = jax.nn.softmax(lg, axis=-1)
    p_group, g_idx = lax.top_k(pg, 1)
    le = (jnp.einsum('bsd,gde->bsge', x, w_re) + b_re).astype(F32)
    le_sel = jnp.take_along_axis(le, g_idx[..., None], axis=2)[:, :, 0]
    top_logits, e_idx = lax.top_k(le_sel, MOE_TOP_K)
    p_exp = jax.nn.softmax(top_logits, axis=-1) * p_group
    onehot_g = jax.nn.one_hot(g_idx[..., 0], MOE_GROUPS, dtype=F32)
    w_e = jnp.sum(jax.nn.one_hot(e_idx, MOE_EXPERTS_PER_GROUP, dtype=F32) * p_exp[..., None], axis=-2)
    combine = onehot_g[..., :, None] * w_e[..., None, :]

    def group_ffn(args):
        wg, wu, wd, cw = args
        h = jax.nn.silu(jnp.einsum('bsd,edf->bsef', x, wg)) * jnp.einsum('bsd,edf->bsef', x, wu)
        return jnp.einsum('bsef,efd->bsd', h * cw[..., None].astype(h.dtype), wd)

    y = lax.map(group_ffn, (w_gate, w_up, w_down, jnp.moveaxis(combine, 2, 0)))
    return jnp.sum(y, axis=0).astype(x.dtype)


def setup_inputs(seed: int = 0) -> dict:
    key = jax.random.key(seed)
    ks = jax.random.split(key, 24)

    def nrm(k, shape, scale):
        return jax.random.normal(k, shape, F32) * scale

    d = D_MODEL
    G, E, Fe = MOE_GROUPS, MOE_EXPERTS_PER_GROUP, MOE_D_EXPERT
    return {
        'x': nrm(ks[0], (BATCH, SEQ, d), 1.0),
        'dn_w_in': nrm(ks[1], (N_DN_LAYERS, d, DN_IN_DIM), d ** -0.5),
        'dn_conv_w': nrm(ks[2], (N_DN_LAYERS, DN_CONV_WIDTH, DN_CONV_DIM), DN_CONV_WIDTH ** -0.5),
        'dn_A_log': jnp.log(jax.random.uniform(ks[3], (N_DN_LAYERS, DN_V_HEADS), F32, 1.0, 16.0)),
        'dn_dt_bias': nrm(ks[4], (N_DN_LAYERS, DN_V_HEADS), 0.1),
        'dn_norm_w': 1.0 + nrm(ks[5], (N_DN_LAYERS, DN_HEAD_DIM), 0.02),
        'dn_w_out': nrm(ks[6], (N_DN_LAYERS, DN_V_DIM, d), DN_V_DIM ** -0.5 * DEEPNORM_BETA),
        'fox_w_in': nrm(ks[7], (N_FOX_LAYERS, d, FOX_IN_DIM), d ** -0.5),
        'fox_b_f': jax.random.uniform(ks[8], (N_FOX_LAYERS, FOX_HEADS), F32, 1.0, 5.0),
        'fox_q_norm_w': 1.0 + nrm(ks[9], (N_FOX_LAYERS, FOX_HEAD_DIM), 0.02),
        'fox_k_norm_w': 1.0 + nrm(ks[10], (N_FOX_LAYERS, FOX_HEAD_DIM), 0.02),
        'fox_w_out': nrm(ks[11], (N_FOX_LAYERS, FOX_DIM, d), FOX_DIM ** -0.5 * DEEPNORM_BETA),
        'ln1_g': 1.0 + nrm(ks[12], (DEPTH, d), 0.02),
        'ln1_b': nrm(ks[13], (DEPTH, d), 0.02),
        'ln2_g': 1.0 + nrm(ks[14], (DEPTH, d), 0.02),
        'ln2_b': nrm(ks[15], (DEPTH, d), 0.02),
        'moe_w_rg': nrm(ks[16], (DEPTH, d, G), d ** -0.5),
        'moe_b_rg': nrm(ks[17], (DEPTH, G), 0.01),
        'moe_w_re': nrm(ks[18], (DEPTH, G, d, E), d ** -0.5),
        'moe_b_re': nrm(ks[19], (DEPTH, G, E), 0.01),
        'moe_w_gate': nrm(ks[20], (DEPTH, G, E, d, Fe), d ** -0.5),
        'moe_w_up': nrm(ks[21], (DEPTH, G, E, d, Fe), d ** -0.5),
        'moe_w_down': nrm(ks[22], (DEPTH, G, E, Fe, d), Fe ** -0.5 * DEEPNORM_BETA),
    }


def reference(x, dn_w_in, dn_conv_w, dn_A_log, dn_dt_bias, dn_norm_w, dn_w_out,
              fox_w_in, fox_b_f, fox_q_norm_w, fox_k_norm_w, fox_w_out,
              ln1_g, ln1_b, ln2_g, ln2_b,
              moe_w_rg, moe_b_rg, moe_w_re, moe_b_re, moe_w_gate, moe_w_up, moe_w_down):
    for i in range(DEPTH):
        j = i // N_MIXERS
        if i % N_MIXERS == 0:
            h = gated_deltanet(x, dn_w_in[j], dn_conv_w[j], dn_A_log[j], dn_dt_bias[j],
                               dn_norm_w[j], dn_w_out[j])
        else:
            h = forgetting_attention(x, fox_w_in[j], fox_b_f[j], fox_q_norm_w[j],
                                     fox_k_norm_w[j], fox_w_out[j])
        x = _layernorm(DEEPNORM_ALPHA * x + h, ln1_g[i], ln1_b[i])
        m = hierarchical_moe(x, moe_w_rg[i], moe_b_rg[i], moe_w_re[i], moe_b_re[i],
                             moe_w_gate[i], moe_w_up[i], moe_w_down[i])
        x = _layernorm(DEEPNORM_ALPHA * x + m, ln2_g[i], ln2_b[i])
    return x
```

```python
import functools
import math

import jax
import jax.numpy as jnp
from jax import lax
from jax.experimental import pallas as pl
from jax.experimental.pallas import tpu as pltpu

F32 = jnp.float32
BF16 = jnp.bfloat16
I32 = jnp.int32

LANES = 128
SUBLANES = 8
V7X_VMEM_BYTES = 64 * 1024 * 1024
VMEM_LIMIT = V7X_VMEM_BYTES - 8 * 1024 * 1024

HEAD_DIM = 128
DN_CHUNK = 64
DN_CONV_WIDTH = 4
MOE_GROUPS = 4
MOE_EXPERTS_PER_GROUP = 8
MOE_TOP_K = 2
NORM_EPS = 1e-6
NEG_BIG = -1e30


def _params(sem, vmem=VMEM_LIMIT):
    return pltpu.CompilerParams(dimension_semantics=sem, vmem_limit_bytes=vmem)


def _tile(dim, pref):
    t = min(dim, pref)
    assert dim % t == 0, (dim, pref)
    return t


def _softplus(x):
    return jnp.maximum(x, 0.0) + jnp.log(1.0 + jnp.exp(-jnp.abs(x)))


def _sigmoid(x):
    return 1.0 / (1.0 + jnp.exp(-x))


def _mm_kernel(a_ref, w_ref, o_ref):
    o_ref[...] = jnp.dot(a_ref[...], w_ref[...].astype(BF16), preferred_element_type=F32)


def matmul(a, w, n_out, *, tm_pref=2048, tn_pref=512):
    m, k = a.shape
    tm = _tile(m, tm_pref)
    tn = _tile(n_out, tn_pref)
    return pl.pallas_call(
        _mm_kernel,
        grid=(m // tm, n_out // tn),
        in_specs=[pl.BlockSpec((tm, k), lambda i, j: (i, 0)),
                  pl.BlockSpec((k, tn), lambda i, j: (0, j))],
        out_specs=pl.BlockSpec((tm, tn), lambda i, j: (i, j)),
        out_shape=jax.ShapeDtypeStruct((m, n_out), F32),
        compiler_params=_params(("parallel", "parallel")),
        name="proj_matmul",
    )(a, w)


def _layernorm_rows(y, g, b):
    mu = jnp.mean(y, axis=1, keepdims=True)
    yc = y - mu
    var = jnp.mean(yc * yc, axis=1, keepdims=True)
    return yc * lax.rsqrt(var + NORM_EPS) * g + b


def _mm_res_ln_kernel(a_ref, w_ref, res_ref, g_ref, b_ref, o_ref, ob_ref, acc_ref, *, alpha):
    k = pl.program_id(1)
    prod = jnp.dot(a_ref[...], w_ref[...].astype(BF16), preferred_element_type=F32)

    @pl.when(k == 0)
    def _():
        acc_ref[...] = prod

    @pl.when(k > 0)
    def _():
        acc_ref[...] += prod

    @pl.when(k == pl.num_programs(1) - 1)
    def _():
        y = alpha * res_ref[...] + acc_ref[...]
        out = _layernorm_rows(y, g_ref[...], b_ref[...])
        o_ref[...] = out
        ob_ref[...] = out.astype(BF16)


def matmul_residual_layernorm(a, w, res, g, b, alpha, *, tm_pref=512, tk_pref=1024):
    m, k = a.shape
    n = w.shape[1]
    tm = _tile(m, tm_pref)
    tk = _tile(k, tk_pref)
    return pl.pallas_call(
        functools.partial(_mm_res_ln_kernel, alpha=alpha),
        grid=(m // tm, k // tk),
        in_specs=[pl.BlockSpec((tm, tk), lambda i, kk: (i, kk)),
                  pl.BlockSpec((tk, n), lambda i, kk: (kk, 0)),
                  pl.BlockSpec((tm, n), lambda i, kk: (i, 0)),
                  pl.BlockSpec((1, n), lambda i, kk: (0, 0)),
                  pl.BlockSpec((1, n), lambda i, kk: (0, 0))],
        out_specs=[pl.BlockSpec((tm, n), lambda i, kk: (i, 0)),
                   pl.BlockSpec((tm, n), lambda i, kk: (i, 0))],
        out_shape=[jax.ShapeDtypeStruct((m, n), F32), jax.ShapeDtypeStruct((m, n), BF16)],
        scratch_shapes=[pltpu.VMEM((tm, n), F32)],
        compiler_params=_params(("parallel", "arbitrary")),
        name="out_proj_res_ln",
    )(a, w, res, g.reshape(1, n), b.reshape(1, n))


def _gdn_pre_kernel(cur_ref, halo_ref, w_ref, o_ref, *, n_qk_blocks, q_scale):
    i = pl.program_id(0)
    j = pl.program_id(1)
    tt, cb = cur_ref.shape
    cur = cur_ref[...]
    keep = (i > 0).astype(F32)
    halo = halo_ref[...] * keep
    ext = jnp.concatenate([halo, cur], axis=0)
    w = w_ref[...]
    acc = ext[SUBLANES:SUBLANES + tt] * w[3:4]
    for kk in range(DN_CONV_WIDTH - 1):
        off = SUBLANES - (DN_CONV_WIDTH - 1) + kk
        acc = acc + ext[off:off + tt] * w[kk:kk + 1]
    y = acc * _sigmoid(acc)
    use_norm = (j < 2 * n_qk_blocks).astype(F32)
    scale = jnp.where(j < n_qk_blocks, q_scale, 1.0).astype(F32)
    for h in range(cb // HEAD_DIM):
        yh = y[:, h * HEAD_DIM:(h + 1) * HEAD_DIM]
        ss = jnp.sum(yh * yh, axis=1, keepdims=True)
        r = lax.rsqrt(ss + NORM_EPS)
        f = (r * use_norm + (1.0 - use_norm)) * scale
        o_ref[:, h * HEAD_DIM:(h + 1) * HEAD_DIM] = yh * f


def gdn_preprocess(proj, conv_w, qk_dim, *, tt_pref=512, cb_pref=512):
    s = proj.shape[0]
    conv_dim = conv_w.shape[1]
    tt = _tile(s, tt_pref)
    cb = _tile(qk_dim, cb_pref)
    assert conv_dim % cb == 0 and tt % SUBLANES == 0
    hb = tt // SUBLANES
    return pl.pallas_call(
        functools.partial(_gdn_pre_kernel, n_qk_blocks=qk_dim // cb, q_scale=HEAD_DIM ** -0.5),
        grid=(s // tt, conv_dim // cb),
        in_specs=[pl.BlockSpec((tt, cb), lambda i, j: (i, j)),
                  pl.BlockSpec((SUBLANES, cb), lambda i, j: (jnp.maximum(i * hb - 1, 0), j)),
                  pl.BlockSpec((DN_CONV_WIDTH, cb), lambda i, j: (0, j))],
        out_specs=pl.BlockSpec((tt, cb), lambda i, j: (i, j)),
        out_shape=jax.ShapeDtypeStruct((s, conv_dim), F32),
        compiler_params=_params(("parallel", "parallel")),
        name="gdn_conv_norm",
    )(proj, proj, conv_w)


def _bmm(a, b):
    return jnp.einsum("nij,njk->nik", a.astype(BF16), b.astype(BF16), preferred_element_type=F32)


def _bmm_nt(a, b):
    return jnp.einsum("nid,njd->nij", a.astype(BF16), b.astype(BF16), preferred_element_type=F32)


def _chunk_cumsum(x, chunk, reverse=False):
    rows = x.shape[0]
    pos = lax.broadcasted_iota(I32, x.shape, 0) % chunk
    step = 1
    while step < chunk:
        if reverse:
            shifted = pltpu.roll(x, rows - step, axis=0)
            x = x + jnp.where(pos + step < chunk, shifted, 0.0)
        else:
            shifted = pltpu.roll(x, step, axis=0)
            x = x + jnp.where(pos >= step, shifted, 0.0)
        step *= 2
    return x


def _delta_kernel(q_ref, k_ref, v_ref, z_ref, ba_ref, alog_ref, dt_ref, nw_ref, o_ref,
                  state_ref, u_ref, wq_ref, attn_ref, kdt_ref, egl_ref, osc_ref, *, hb):
    t = pl.program_id(1)
    tb = q_ref.shape[0]
    c = DN_CHUNK
    nc = tb // c
    d = HEAD_DIM

    @pl.when(t == 0)
    def _():
        state_ref[...] = jnp.zeros_like(state_ref)

    ba = ba_ref[...]
    beta_all = _sigmoid(ba)
    g_all = -jnp.exp(alog_ref[...]) * _softplus(ba + dt_ref[...])
    big_g = _chunk_cumsum(g_all, c)
    tail_g = _chunk_cumsum(g_all, c, reverse=True) - g_all
    e_g = jnp.exp(big_g)
    e_tail = jnp.exp(tail_g)
    e_tot = jnp.exp(big_g + tail_g)
    big_g_t = big_g.T

    row = lax.broadcasted_iota(I32, (c, c), 0)
    col = lax.broadcasted_iota(I32, (c, c), 1)
    causal = row >= col
    strict = row > col
    eye = (row == col).astype(F32)

    for hh in range(hb):
        qh = q_ref[:, (hh // 2) * d:(hh // 2 + 1) * d]
        kh = k_ref[:, (hh // 2) * d:(hh // 2 + 1) * d]
        vh = v_ref[:, hh * d:(hh + 1) * d]
        gi = hb + hh
        bcol = beta_all[:, hh:hh + 1]
        egcol = e_g[:, gi:gi + 1]
        gcol_b = jnp.broadcast_to(big_g[:, gi:gi + 1], (tb, c))
        decays = []
        for n in range(nc):
            diff = gcol_b[n * c:(n + 1) * c, :] - big_g_t[gi:gi + 1, n * c:(n + 1) * c]
            decays.append(jnp.where(causal, jnp.exp(jnp.where(causal, diff, 0.0)), 0.0))
        decay = jnp.stack(decays, axis=0)
        kb = kh * bcol
        q3 = qh.reshape(nc, c, d)
        k3 = kh.reshape(nc, c, d)
        kb3 = kb.reshape(nc, c, d)
        vb3 = (vh * bcol).reshape(nc, c, d)
        lower = jnp.where(strict[None], _bmm_nt(kb3, k3) * decay, 0.0)
        attn = _bmm_nt(q3, k3) * decay
        tinv = eye[None] - lower
        power = lower
        span = 2
        while span < c:
            power = _bmm(power, power)
            tinv = tinv + _bmm(tinv, power)
            span *= 2
        rhs = jnp.concatenate([vb3, (kb * egcol).reshape(nc, c, d)], axis=-1)
        sol = _bmm(tinv, rhs)
        u_ref[hh] = sol[..., :d]
        wq_ref[hh] = jnp.concatenate([sol[..., d:], (qh * egcol).reshape(nc, c, d)], axis=1).astype(BF16)
        attn_ref[hh] = attn.astype(BF16)
        kd_t = (kh * e_tail[:, gi:gi + 1]).T.astype(BF16)
        for n in range(nc):
            kdt_ref[hh, n] = kd_t[:, n * c:(n + 1) * c]
        egl_ref[hh] = jnp.broadcast_to(e_tot[:, gi:gi + 1], (tb, d))

    def chunk_step(n, carry):
        r0 = pl.multiple_of(n * c, c)
        for hh in range(hb):
            s_old = state_ref[hh]
            ws = jnp.dot(wq_ref[hh, n], s_old.astype(BF16), preferred_element_type=F32)
            v_new = u_ref[hh, n] - ws[:c]
            v_new_b = v_new.astype(BF16)
            o = ws[c:] + jnp.dot(attn_ref[hh, n], v_new_b, preferred_element_type=F32)
            state_ref[hh] = s_old * egl_ref[hh, pl.ds(r0, 1), :] + jnp.dot(kdt_ref[hh, n], v_new_b,
                                                                            preferred_element_type=F32)
            osc_ref[pl.ds(r0, c), hh * d:(hh + 1) * d] = o
        return carry

    lax.fori_loop(0, nc, chunk_step, 0)

    nw = nw_ref[...]
    for hh in range(hb):
        o = osc_ref[:, hh * d:(hh + 1) * d]
        z = z_ref[:, hh * d:(hh + 1) * d]
        ms = jnp.mean(o * o, axis=1, keepdims=True)
        y = o * lax.rsqrt(ms + NORM_EPS) * nw * (z * _sigmoid(z))
        o_ref[:, hh * d:(hh + 1) * d] = y.astype(BF16)


def gdn_delta_rule(qkv, proj, ba_g, alog_g, dt_g, norm_w, n_v_heads, *, hb=4, tb_pref=512):
    s = qkv.shape[0]
    d = HEAD_DIM
    qk_dim = (n_v_heads // 2) * d
    v_dim = n_v_heads * d
    tb = _tile(s, tb_pref)
    nc = tb // DN_CHUNK
    hg = n_v_heads // hb
    qw = (hb // 2) * d
    vw = hb * d
    return pl.pallas_call(
        functools.partial(_delta_kernel, hb=hb),
        grid=(hg, s // tb),
        in_specs=[pl.BlockSpec((tb, qw), lambda g, t: (t, g)),
                  pl.BlockSpec((tb, qw), lambda g, t: (t, qk_dim // qw + g)),
                  pl.BlockSpec((tb, vw), lambda g, t: (t, 2 * qk_dim // vw + g)),
                  pl.BlockSpec((tb, vw), lambda g, t: (t, (2 * qk_dim + v_dim) // vw + g)),
                  pl.BlockSpec((tb, LANES), lambda g, t: (t, g)),
                  pl.BlockSpec((None, 1, LANES), lambda g, t: (g, 0, 0)),
                  pl.BlockSpec((None, 1, LANES), lambda g, t: (g, 0, 0)),
                  pl.BlockSpec((1, d), lambda g, t: (0, 0))],
        out_specs=pl.BlockSpec((tb, vw), lambda g, t: (t, g)),
        out_shape=jax.ShapeDtypeStruct((s, v_dim), BF16),
        scratch_shapes=[pltpu.VMEM((hb, d, d), F32),
                        pltpu.VMEM((hb, nc, DN_CHUNK, d), F32),
                        pltpu.VMEM((hb, nc, 2 * DN_CHUNK, d), BF16),
                        pltpu.VMEM((hb, nc, DN_CHUNK, DN_CHUNK), BF16),
                        pltpu.VMEM((hb, nc, d, DN_CHUNK), BF16),
                        pltpu.VMEM((hb, tb, d), F32),
                        pltpu.VMEM((tb, vw), F32)],
        compiler_params=_params(("parallel", "arbitrary")),
        name="gdn_delta_rule",
    )(qkv, qkv, qkv, proj, ba_g, alog_g, dt_g, norm_w.reshape(1, d))


def _fox_pre_kernel(x_ref, qw_ref, kw_ref, o_ref, *, n_head_blocks):
    j = pl.program_id(1)
    cb = x_ref.shape[1]
    x = x_ref[...]
    is_q = (j < n_head_blocks).astype(F32)
    use_norm = (j < 2 * n_head_blocks).astype(F32)
    wsel = qw_ref[...] * is_q + kw_ref[...] * (1.0 - is_q)
    wsel = wsel * use_norm + (1.0 - use_norm)
    for h in range(cb // HEAD_DIM):
        xh = x[:, h * HEAD_DIM:(h + 1) * HEAD_DIM]
        ms = jnp.mean(xh * xh, axis=1, keepdims=True)
        r = lax.rsqrt(ms + NORM_EPS)
        f = r * use_norm + (1.0 - use_norm)
        o_ref[:, h * HEAD_DIM:(h + 1) * HEAD_DIM] = (xh * f * wsel).astype(BF16)


def fox_preprocess(proj, q_norm_w, k_norm_w, dim, *, tt_pref=512, cb_pref=512):
    s = proj.shape[0]
    tt = _tile(s, tt_pref)
    cb = _tile(dim, cb_pref)
    return pl.pallas_call(
        functools.partial(_fox_pre_kernel, n_head_blocks=dim // cb),
        grid=(s // tt, 3 * dim // cb),
        in_specs=[pl.BlockSpec((tt, cb), lambda i, j: (i, j)),
                  pl.BlockSpec((1, HEAD_DIM), lambda i, j: (0, 0)),
                  pl.BlockSpec((1, HEAD_DIM), lambda i, j: (0, 0))],
        out_specs=pl.BlockSpec((tt, cb), lambda i, j: (i, j)),
        out_shape=jax.ShapeDtypeStruct((s, 3 * dim), BF16),
        compiler_params=_params(("parallel", "parallel")),
        name="fox_qk_norm",
    )(proj, q_norm_w.reshape(1, HEAD_DIM), k_norm_w.reshape(1, HEAD_DIM))


def _fox_cumgate_kernel(f_ref, b_ref, o_ref, carry_ref):
    i = pl.program_id(0)
    tt = f_ref.shape[0]

    @pl.when(i == 0)
    def _():
        carry_ref[...] = jnp.zeros_like(carry_ref)

    x = f_ref[...] + b_ref[...]
    log_f = jnp.minimum(x, 0.0) - jnp.log(1.0 + jnp.exp(-jnp.abs(x)))
    csum = _chunk_cumsum(log_f, tt) + carry_ref[...]
    o_ref[...] = csum
    carry_ref[...] = csum[tt - 1:tt, :]


def fox_cumulative_gate(f_logits, b_row, *, tt_pref=512):
    s = f_logits.shape[0]
    tt = _tile(s, tt_pref)
    return pl.pallas_call(
        _fox_cumgate_kernel,
        grid=(s // tt,),
        in_specs=[pl.BlockSpec((tt, LANES), lambda i: (i, 0)),
                  pl.BlockSpec((1, LANES), lambda i: (0, 0))],
        out_specs=pl.BlockSpec((tt, LANES), lambda i: (i, 0)),
        out_shape=jax.ShapeDtypeStruct((s, LANES), F32),
        scratch_shapes=[pltpu.VMEM((1, LANES), F32)],
        compiler_params=_params(("arbitrary",)),
        name="fox_cumulative_gate",
    )(f_logits, b_row)


def _fox_attn_kernel(q_ref, k_ref, v_ref, c_ref, ct_ref, gate_ref, o_ref, m_ref, l_ref, acc_ref, *, tk):
    h = pl.program_id(0)
    qi = pl.program_id(1)
    tq = q_ref.shape[0]
    scale = HEAD_DIM ** -0.5
    q = q_ref[...]
    lane = lax.broadcasted_iota(I32, c_ref.shape, 1)
    ci = jnp.sum(jnp.where(lane == h, c_ref[...], 0.0), axis=1, keepdims=True)

    m_ref[...] = jnp.full_like(m_ref, NEG_BIG)
    l_ref[...] = jnp.zeros_like(l_ref)
    acc_ref[...] = jnp.zeros_like(acc_ref)

    def block(kj, masked):
        k0 = pl.multiple_of(kj * tk, tk)
        kb = k_ref[pl.ds(k0, tk), :]
        vb = v_ref[pl.ds(k0, tk), :]
        s = lax.dot_general(q, kb, (((1,), (1,)), ((), ())), preferred_element_type=F32) * scale
        cj = ct_ref[:, pl.ds(k0, tk)]
        s = s + (ci - cj)
        if masked:
            rpos = qi * tq + lax.broadcasted_iota(I32, (tq, tk), 0)
            cpos = kj * tk + lax.broadcasted_iota(I32, (tq, tk), 1)
            s = jnp.where(rpos >= cpos, s, -jnp.inf)
        m_old = m_ref[...]
        m_new = jnp.maximum(m_old, jnp.max(s, axis=1, keepdims=True))
        alpha = jnp.exp(m_old - m_new)
        p = jnp.exp(s - m_new)
        l_ref[...] = alpha * l_ref[...] + jnp.sum(p, axis=1, keepdims=True)
        acc_ref[...] = alpha * acc_ref[...] + jnp.dot(p.astype(BF16), vb, preferred_element_type=F32)
        m_ref[...] = m_new

    n_full = (qi * tq) // tk

    def full_body(kj, carry):
        block(kj, False)
        return carry

    lax.fori_loop(0, n_full, full_body, 0)
    for dj in range(tq // tk):
        block(n_full + dj, True)

    o = acc_ref[...] / l_ref[...]
    o_ref[...] = (o * _sigmoid(gate_ref[...])).astype(BF16)


def fox_attention(qkv_n, c_pad, c_t, proj, n_heads, *, tq_pref=512, tk_pref=512):
    s = qkv_n.shape[0]
    d = HEAD_DIM
    tq = _tile(s, tq_pref)
    tk = _tile(tq, tk_pref)
    return pl.pallas_call(
        functools.partial(_fox_attn_kernel, tk=tk),
        grid=(n_heads, s // tq),
        in_specs=[pl.BlockSpec((tq, d), lambda h, i: (i, h)),
                  pl.BlockSpec((s, d), lambda h, i: (0, n_heads + h)),
                  pl.BlockSpec((s, d), lambda h, i: (0, 2 * n_heads + h)),
                  pl.BlockSpec((tq, LANES), lambda h, i: (i, 0)),
                  pl.BlockSpec((None, 1, s), lambda h, i: (h, 0, 0)),
                  pl.BlockSpec((tq, d), lambda h, i: (i, 3 * n_heads + h))],
        out_specs=pl.BlockSpec((tq, d), lambda h, i: (i, h)),
        out_shape=jax.ShapeDtypeStruct((s, n_heads * d), BF16),
        scratch_shapes=[pltpu.VMEM((tq, 1), F32), pltpu.VMEM((tq, 1), F32), pltpu.VMEM((tq, d), F32)],
        compiler_params=_params(("parallel", "arbitrary")),
        name="fox_attention",
    )(qkv_n, qkv_n, qkv_n, c_pad, c_t, proj)


def _split_bf16(x):
    hi = x.astype(BF16)
    lo = (x - hi.astype(F32)).astype(BF16)
    return hi, lo


def _first_argmax(vals, mask, lane_f):
    v = jnp.where(mask, vals, NEG_BIG)
    vmax = jnp.max(v, axis=1, keepdims=True)
    hit = jnp.logical_and(mask, v == vmax)
    idx = jnp.min(jnp.where(hit, lane_f, float(LANES)), axis=1, keepdims=True)
    return vmax, idx


def _router_kernel(x_ref, w_ref, b_ref, ids_ref, wts_ref, rank_ref, cnt_ref, carry_ref, *, n_groups, n_exp):
    i = pl.program_id(0)
    tm = x_ref.shape[0]

    @pl.when(i == 0)
    def _():
        carry_ref[...] = jnp.zeros_like(carry_ref)

    xh, xl = _split_bf16(x_ref[...])
    wh, wl = _split_bf16(w_ref[...])
    lg = (jnp.dot(xh, wh, preferred_element_type=F32) + jnp.dot(xh, wl, preferred_element_type=F32)
          + jnp.dot(xl, wh, preferred_element_type=F32)) + b_ref[...]
    lane = lax.broadcasted_iota(I32, lg.shape, 1)
    lane_f = lane.astype(F32)

    gmask = lane < n_groups
    gmax, gidx = _first_argmax(lg, gmask, lane_f)
    gsum = jnp.sum(jnp.where(gmask, jnp.exp(jnp.where(gmask, lg, NEG_BIG) - gmax), 0.0), axis=1, keepdims=True)
    p_group = 1.0 / gsum

    e_lo = float(n_groups) + gidx * float(n_exp)
    emask = jnp.logical_and(lane_f >= e_lo, lane_f < e_lo + float(n_exp))
    t1, i1 = _first_argmax(lg, emask, lane_f)
    emask2 = jnp.logical_and(emask, lane_f != i1)
    t2, i2 = _first_argmax(lg, emask2, lane_f)
    e21 = jnp.exp(t2 - t1)
    p1 = p_group / (1.0 + e21)
    p2 = p_group * e21 / (1.0 + e21)
    id1 = i1 - float(n_groups)
    id2 = i2 - float(n_groups)

    oh = jnp.where(lane_f == id1, 1.0, 0.0) + jnp.where(lane_f == id2, 1.0, 0.0)
    r = lax.broadcasted_iota(I32, (tm, tm), 0)
    cc = lax.broadcasted_iota(I32, (tm, tm), 1)
    tril = jnp.where(r > cc, 1.0, 0.0).astype(BF16)
    before = jnp.dot(tril, oh.astype(BF16), preferred_element_type=F32) + carry_ref[...]
    rank1 = jnp.sum(jnp.where(lane_f == id1, before, 0.0), axis=1, keepdims=True)
    rank2 = jnp.sum(jnp.where(lane_f == id2, before, 0.0), axis=1, keepdims=True)
    new_carry = carry_ref[...] + jnp.sum(oh, axis=0, keepdims=True)
    carry_ref[...] = new_carry
    cnt_ref[...] = jnp.broadcast_to(new_carry, cnt_ref.shape).astype(I32)

    first = lane == 0
    second = lane == 1
    ids_ref[...] = jnp.where(first, id1, jnp.where(second, id2, 0.0)).astype(I32)
    wts_ref[...] = jnp.where(first, p1, jnp.where(second, p2, 0.0))
    rank_ref[...] = jnp.where(first, rank1, jnp.where(second, rank2, 0.0)).astype(I32)


def moe_router(x, w_pad, b_pad, n_groups, n_exp, *, tm_pref=512):
    s, dm = x.shape
    tm = _tile(s, tm_pref)
    return pl.pallas_call(
        functools.partial(_router_kernel, n_groups=n_groups, n_exp=n_exp),
        grid=(s // tm,),
        in_specs=[pl.BlockSpec((tm, dm), lambda i: (i, 0)),
                  pl.BlockSpec((dm, LANES), lambda i: (0, 0)),
                  pl.BlockSpec((1, LANES), lambda i: (0, 0))],
        out_specs=[pl.BlockSpec((tm, LANES), lambda i: (i, 0)),
                   pl.BlockSpec((tm, LANES), lambda i: (i, 0)),
                   pl.BlockSpec((tm, LANES), lambda i: (i, 0)),
                   pl.BlockSpec((SUBLANES, LANES), lambda i: (0, 0))],
        out_shape=[jax.ShapeDtypeStruct((s, LANES), I32), jax.ShapeDtypeStruct((s, LANES), F32),
                   jax.ShapeDtypeStruct((s, LANES), I32), jax.ShapeDtypeStruct((SUBLANES, LANES), I32)],
        scratch_shapes=[pltpu.VMEM((1, LANES), F32)],
        compiler_params=_params(("arbitrary",)),
        name="moe_router",
    )(x, w_pad, b_pad)


def _gather_rows(src_hbm, dst, sem, idx_ref, base, n_rows):
    def body(r, carry):
        tok = idx_ref[base + r]
        pltpu.make_async_copy(src_hbm.at[pl.ds(tok, 1), :], dst.at[pl.ds(r, 1), :], sem).start()
        return carry

    lax.fori_loop(0, n_rows, body, 0)


def _wait_rows(src_hbm, dst, sem, n_rows):
    def body(r, carry):
        pltpu.make_async_copy(src_hbm.at[pl.ds(0, 1), :], dst.at[pl.ds(r, 1), :], sem).wait()
        return carry

    lax.fori_loop(0, n_rows, body, 0)


def _moe_ffn_kernel(te_ref, nu_ref, src_ref, x_hbm, wg_ref, wu_ref, wd_ref, o_ref,
                    xbuf, sem, wgb, wub, wdb):
    i = pl.program_id(0)
    tm = o_ref.shape[0]
    n_used = nu_ref[0]
    slot = i % 2

    @pl.when(i == 0)
    def _():
        _gather_rows(x_hbm, xbuf.at[0], sem.at[0], src_ref, 0, tm)

    @pl.when(i + 1 < n_used)
    def _():
        _gather_rows(x_hbm, xbuf.at[1 - slot], sem.at[1 - slot], src_ref, (i + 1) * tm, tm)

    @pl.when(i < n_used)
    def _():
        is_new = jnp.logical_or(i == 0, te_ref[i] != te_ref[jnp.maximum(i - 1, 0)])

        @pl.when(is_new)
        def _():
            wgb[...] = wg_ref[...].astype(BF16)
            wub[...] = wu_ref[...].astype(BF16)
            wdb[...] = wd_ref[...].astype(BF16)

        _wait_rows(x_hbm, xbuf.at[slot], sem.at[slot], tm)
        x = xbuf[slot].astype(BF16)
        gate = jnp.dot(x, wgb[...], preferred_element_type=F32)
        up = jnp.dot(x, wub[...], preferred_element_type=F32)
        hmid = (gate * _sigmoid(gate) * up).astype(BF16)
        o_ref[...] = jnp.dot(hmid, wdb[...], preferred_element_type=F32)

    @pl.when(i >= n_used)
    def _():
        o_ref[...] = jnp.zeros_like(o_ref)


def moe_expert_ffn(x, w_gate, w_up, w_down, layer, tile_expert, n_used, src_tok, tm):
    s, dm = x.shape
    n_exp_total = MOE_GROUPS * MOE_EXPERTS_PER_GROUP
    fe = w_gate.shape[-1]
    n_tiles = tile_expert.shape[0]
    wg = w_gate.reshape(-1, dm, fe)
    wu = w_up.reshape(-1, dm, fe)
    wd = w_down.reshape(-1, fe, dm)
    base = layer * n_exp_total
    grid_spec = pltpu.PrefetchScalarGridSpec(
        num_scalar_prefetch=3,
        grid=(n_tiles,),
        in_specs=[pl.BlockSpec(memory_space=pl.ANY),
                  pl.BlockSpec((None, dm, fe), lambda i, te, nu, src: (base + te[i], 0, 0)),
                  pl.BlockSpec((None, dm, fe), lambda i, te, nu, src: (base + te[i], 0, 0)),
                  pl.BlockSpec((None, fe, dm), lambda i, te, nu, src: (base + te[i], 0, 0))],
        out_specs=pl.BlockSpec((tm, dm), lambda i, te, nu, src: (i, 0)),
        scratch_shapes=[pltpu.VMEM((2, tm, dm), F32),
                        pltpu.SemaphoreType.DMA((2,)),
                        pltpu.VMEM((dm, fe), BF16),
                        pltpu.VMEM((dm, fe), BF16),
                        pltpu.VMEM((fe, dm), BF16)])
    return pl.pallas_call(
        _moe_ffn_kernel,
        grid_spec=grid_spec,
        out_shape=jax.ShapeDtypeStruct((n_tiles * tm, dm), F32),
        compiler_params=_params(("arbitrary",)),
        name="moe_expert_ffn",
    )(tile_expert, n_used, src_tok, x, wg, wu, wd)


def _moe_combine_kernel(pos_ref, y_hbm, x_ref, wts_ref, g_ref, b_ref, o_ref, ob_ref, ybuf, sem, *, alpha):
    i = pl.program_id(0)
    n = pl.num_programs(0)
    tm = x_ref.shape[0]
    slot = i % 2

    def gather(tile, sl):
        for kk in range(MOE_TOP_K):
            _gather_rows(y_hbm, ybuf.at[sl, kk], sem.at[sl], pos_ref, (kk * n + tile) * tm, tm)

    @pl.when(i == 0)
    def _():
        gather(0, 0)

    @pl.when(i + 1 < n)
    def _():
        gather(i + 1, 1 - slot)

    for kk in range(MOE_TOP_K):
        _wait_rows(y_hbm, ybuf.at[slot, kk], sem.at[slot], tm)
    wts = wts_ref[...]
    mix = wts[:, 0:1] * ybuf[slot, 0] + wts[:, 1:2] * ybuf[slot, 1]
    y = alpha * x_ref[...] + mix
    out = _layernorm_rows(y, g_ref[...], b_ref[...])
    o_ref[...] = out
    ob_ref[...] = out.astype(BF16)


def moe_combine_layernorm(y_sorted, pos_km, x, wts, g, b, alpha, *, tm_pref=256):
    s, dm = x.shape
    tm = _tile(s, tm_pref)
    grid_spec = pltpu.PrefetchScalarGridSpec(
        num_scalar_prefetch=1,
        grid=(s // tm,),
        in_specs=[pl.BlockSpec(memory_space=pl.ANY),
                  pl.BlockSpec((tm, dm), lambda i, pos: (i, 0)),
                  pl.BlockSpec((tm, LANES), lambda i, pos: (i, 0)),
                  pl.BlockSpec((1, dm), lambda i, pos: (0, 0)),
                  pl.BlockSpec((1, dm), lambda i, pos: (0, 0))],
        out_specs=[pl.BlockSpec((tm, dm), lambda i, pos: (i, 0)),
                   pl.BlockSpec((tm, dm), lambda i, pos: (i, 0))],
        scratch_shapes=[pltpu.VMEM((2, MOE_TOP_K, tm, dm), F32),
                        pltpu.SemaphoreType.DMA((2,))])
    return pl.pallas_call(
        functools.partial(_moe_combine_kernel, alpha=alpha),
        grid_spec=grid_spec,
        out_shape=[jax.ShapeDtypeStruct((s, dm), F32), jax.ShapeDtypeStruct((s, dm), BF16)],
        compiler_params=_params(("arbitrary",)),
        name="moe_combine_ln",
    )(pos_km, y_sorted, x, wts, g.reshape(1, dm), b.reshape(1, dm))


def _pad_cols(w, width=LANES):
    return jnp.pad(w, ((0, 0), (0, width - w.shape[1])))


def _moe_layer(x, x_bf16, layer, w_rg, b_rg, w_re, b_re, w_gate, w_up, w_down, ln_g, ln_b, alpha, *, tm=256):
    del x_bf16
    s, dm = x.shape
    n_groups, _, n_exp = w_re.shape
    n_total = n_groups * n_exp
    w_r = _pad_cols(jnp.concatenate([w_rg, jnp.transpose(w_re, (1, 0, 2)).reshape(dm, n_total)], axis=1))
    b_r = _pad_cols(jnp.concatenate([b_rg, b_re.reshape(n_total)])[None, :])
    ids, wts, ranks, counts = moe_router(x, w_r, b_r, n_groups, n_exp)

    n_tiles = (s * MOE_TOP_K) // tm + n_total
    counts = counts[0, :n_total]
    padded = ((counts + tm - 1) // tm) * tm
    ends = jnp.cumsum(padded)
    starts = ends - padded
    ids2 = ids[:, :MOE_TOP_K]
    pos = starts[ids2] + ranks[:, :MOE_TOP_K]
    tok = jnp.broadcast_to(jnp.arange(s, dtype=I32)[:, None], (s, MOE_TOP_K))
    src_tok = jnp.zeros((n_tiles * tm,), I32).at[pos.reshape(-1)].set(tok.reshape(-1))
    n_used = (ends[-1] // tm).astype(I32)
    tile_start = jnp.arange(n_tiles, dtype=I32) * tm
    tile_expert = jnp.minimum(jnp.searchsorted(ends, tile_start, side="right"), n_total - 1).astype(I32)
    last_expert = tile_expert[jnp.maximum(n_used - 1, 0)]
    tile_expert = jnp.where(jnp.arange(n_tiles) < n_used, tile_expert, last_expert)

    y_sorted = moe_expert_ffn(x, w_gate, w_up, w_down, layer, tile_expert, n_used.reshape(1), src_tok, tm)
    pos_km = jnp.transpose(pos.reshape(s // tm, tm, MOE_TOP_K), (2, 0, 1)).reshape(-1).astype(I32)
    return moe_combine_layernorm(y_sorted, pos_km, x, wts, ln_g, ln_b, alpha, tm_pref=tm)


def kernel(x, dn_w_in, dn_conv_w, dn_A_log, dn_dt_bias, dn_norm_w, dn_w_out, fox_w_in, fox_b_f, fox_q_norm_w,
           fox_k_norm_w, fox_w_out, ln1_g, ln1_b, ln2_g, ln2_b, moe_w_rg, moe_b_rg, moe_w_re, moe_b_re,
           moe_w_gate, moe_w_up, moe_w_down):
    batch, s, dm = x.shape
    assert batch == 1
    depth = ln1_g.shape[0]
    alpha = (2 * depth) ** 0.25
    x2d = x.reshape(s, dm)
    xb = x2d.astype(BF16)
    d = HEAD_DIM
    hb = 4

    for i in range(depth):
        j = i // 2
        if i % 2 == 0:
            n_v_heads = dn_A_log.shape[1]
            qk_dim = (n_v_heads // 2) * d
            v_dim = n_v_heads * d
            conv_dim = 2 * qk_dim + v_dim
            w_in = dn_w_in[j]
            proj = matmul(xb, w_in, conv_dim + v_dim)
            hg = n_v_heads // hb
            w_b = w_in[:, conv_dim + v_dim:conv_dim + v_dim + n_v_heads].reshape(dm, hg, hb)
            w_a = w_in[:, conv_dim + v_dim + n_v_heads:].reshape(dm, hg, hb)
            w_ba = jnp.pad(jnp.concatenate([w_b, w_a], axis=2), ((0, 0), (0, 0), (0, LANES - 2 * hb)))
            ba_g = matmul(xb, w_ba.reshape(dm, hg * LANES), hg * LANES)
            zeros = jnp.zeros((hg, hb), F32)
            alog_g = jnp.pad(jnp.concatenate([zeros, dn_A_log[j].reshape(hg, hb)], axis=1),
                             ((0, 0), (0, LANES - 2 * hb))).reshape(hg, 1, LANES)
            dt_g = jnp.pad(jnp.concatenate([zeros, dn_dt_bias[j].reshape(hg, hb)], axis=1),
                           ((0, 0), (0, LANES - 2 * hb))).reshape(hg, 1, LANES)
            qkv = gdn_preprocess(proj, dn_conv_w[j], qk_dim)
            mixed = gdn_delta_rule(qkv, proj, ba_g, alog_g, dt_g, dn_norm_w[j], n_v_heads, hb=hb)
            w_out = dn_w_out[j]
        else:
            n_heads = fox_b_f.shape[1]
            dim = n_heads * d
            w_in = fox_w_in[j]
            proj = matmul(xb, w_in, 4 * dim)
            f_logits = matmul(xb, _pad_cols(w_in[:, 4 * dim:]), LANES, tn_pref=LANES)
            qkv_n = fox_preprocess(proj, fox_q_norm_w[j], fox_k_norm_w[j], dim)
            c_pad = fox_cumulative_gate(f_logits, _pad_cols(fox_b_f[j][None, :]))
            c_t = jnp.transpose(c_pad[:, :n_heads]).reshape(n_heads, 1, s)
            mixed = fox_attention(qkv_n, c_pad, c_t, proj, n_heads)
            w_out = fox_w_out[j]
        x2d, xb = matmul_residual_layernorm(mixed, w_out, x2d, ln1_g[i], ln1_b[i], alpha)
        x2d, xb = _moe_layer(x2d, xb, i, moe_w_rg[i], moe_b_rg[i], moe_w_re[i], moe_b_re[i],
                             moe_w_gate, moe_w_up, moe_w_down, ln2_g[i], ln2_b[i], alpha)
    return x2d.reshape(batch, s, dm)
```

```python
import functools
import math

import jax
import jax.numpy as jnp
from jax import lax
from jax.experimental import pallas as pl
from jax.experimental.pallas import tpu as pltpu

F32 = jnp.float32
BF16 = jnp.bfloat16
I32 = jnp.int32

LANES = 128
SUBLANES = 8
V7X_VMEM_BYTES = 64 * 1024 * 1024
VMEM_LIMIT = V7X_VMEM_BYTES - 8 * 1024 * 1024

HEAD_DIM = 128
SOLVE_CHUNK = 128
DN_CONV_WIDTH = 4
MOE_GROUPS = 4
MOE_EXPERTS_PER_GROUP = 8
MOE_TOP_K = 2
NORM_EPS = 1e-6
NEG_BIG = -1e30
LOG2E = 1.4426950408889634


def _params(sem, vmem=VMEM_LIMIT):
    return pltpu.CompilerParams(dimension_semantics=sem, vmem_limit_bytes=vmem)


def _tile(dim, pref):
    t = min(dim, pref)
    assert dim % t == 0, (dim, pref)
    return t


def _softplus(x):
    return jnp.maximum(x, 0.0) + jnp.log(1.0 + jnp.exp(-jnp.abs(x)))


def _sigmoid(x):
    return 1.0 / (1.0 + jnp.exp(-x))


def _mm_kernel(a_ref, w_ref, o_ref):
    o_ref[...] = jnp.dot(a_ref[...], w_ref[...].astype(BF16), preferred_element_type=F32)


def matmul(a, w, n_out, *, tm_pref=2048, tn_pref=512):
    m, k = a.shape
    tm = _tile(m, tm_pref)
    tn = _tile(n_out, tn_pref)
    return pl.pallas_call(
        _mm_kernel,
        grid=(m // tm, n_out // tn),
        in_specs=[pl.BlockSpec((tm, k), lambda i, j: (i, 0)),
                  pl.BlockSpec((k, tn), lambda i, j: (0, j))],
        out_specs=pl.BlockSpec((tm, tn), lambda i, j: (i, j)),
        out_shape=jax.ShapeDtypeStruct((m, n_out), F32),
        compiler_params=_params(("parallel", "parallel")),
        name="proj_matmul",
    )(a, w)


def _layernorm_rows(y, g, b):
    mu = jnp.mean(y, axis=1, keepdims=True)
    yc = y - mu
    var = jnp.mean(yc * yc, axis=1, keepdims=True)
    return yc * lax.rsqrt(var + NORM_EPS) * g + b


def _mm_res_ln_kernel(a_ref, w_ref, res_ref, g_ref, b_ref, o_ref, ob_ref, acc_ref, *, alpha):
    k = pl.program_id(1)
    prod = jnp.dot(a_ref[...], w_ref[...].astype(BF16), preferred_element_type=F32)

    @pl.when(k == 0)
    def _():
        acc_ref[...] = prod

    @pl.when(k > 0)
    def _():
        acc_ref[...] += prod

    @pl.when(k == pl.num_programs(1) - 1)
    def _():
        y = alpha * res_ref[...] + acc_ref[...]
        out = _layernorm_rows(y, g_ref[...], b_ref[...])
        o_ref[...] = out
        ob_ref[...] = out.astype(BF16)


def matmul_residual_layernorm(a, w, res, g, b, alpha, *, tm_pref=512, tk_pref=1024):
    m, k = a.shape
    n = w.shape[1]
    tm = _tile(m, tm_pref)
    tk = _tile(k, tk_pref)
    return pl.pallas_call(
        functools.partial(_mm_res_ln_kernel, alpha=alpha),
        grid=(m // tm, k // tk),
        in_specs=[pl.BlockSpec((tm, tk), lambda i, kk: (i, kk)),
                  pl.BlockSpec((tk, n), lambda i, kk: (kk, 0)),
                  pl.BlockSpec((tm, n), lambda i, kk: (i, 0)),
                  pl.BlockSpec((1, n), lambda i, kk: (0, 0)),
                  pl.BlockSpec((1, n), lambda i, kk: (0, 0))],
        out_specs=[pl.BlockSpec((tm, n), lambda i, kk: (i, 0)),
                   pl.BlockSpec((tm, n), lambda i, kk: (i, 0))],
        out_shape=[jax.ShapeDtypeStruct((m, n), F32), jax.ShapeDtypeStruct((m, n), BF16)],
        scratch_shapes=[pltpu.VMEM((tm, n), F32)],
        compiler_params=_params(("parallel", "arbitrary")),
        name="out_proj_res_ln",
    )(a, w, res, g.reshape(1, n), b.reshape(1, n))


def _gdn_pre_kernel(cur_ref, halo_ref, w_ref, o_ref, *, n_qk_blocks, q_scale):
    i = pl.program_id(0)
    j = pl.program_id(1)
    tt, cb = cur_ref.shape
    cur = cur_ref[...]
    keep = (i > 0).astype(F32)
    halo = halo_ref[...] * keep
    ext = jnp.concatenate([halo, cur], axis=0)
    w = w_ref[...]
    acc = ext[SUBLANES:SUBLANES + tt] * w[3:4]
    for kk in range(DN_CONV_WIDTH - 1):
        off = SUBLANES - (DN_CONV_WIDTH - 1) + kk
        acc = acc + ext[off:off + tt] * w[kk:kk + 1]
    y = acc * _sigmoid(acc)
    use_norm = (j < 2 * n_qk_blocks).astype(F32)
    scale = jnp.where(j < n_qk_blocks, q_scale, 1.0).astype(F32)
    for h in range(cb // HEAD_DIM):
        yh = y[:, h * HEAD_DIM:(h + 1) * HEAD_DIM]
        ss = jnp.sum(yh * yh, axis=1, keepdims=True)
        r = lax.rsqrt(ss + NORM_EPS)
        f = (r * use_norm + (1.0 - use_norm)) * scale
        o_ref[:, h * HEAD_DIM:(h + 1) * HEAD_DIM] = yh * f


def gdn_preprocess(proj, conv_w, qk_dim, *, tt_pref=512, cb_pref=512):
    s = proj.shape[0]
    conv_dim = conv_w.shape[1]
    tt = _tile(s, tt_pref)
    cb = _tile(qk_dim, cb_pref)
    assert conv_dim % cb == 0 and tt % SUBLANES == 0
    hb = tt // SUBLANES
    return pl.pallas_call(
        functools.partial(_gdn_pre_kernel, n_qk_blocks=qk_dim // cb, q_scale=HEAD_DIM ** -0.5),
        grid=(s // tt, conv_dim // cb),
        in_specs=[pl.BlockSpec((tt, cb), lambda i, j: (i, j)),
                  pl.BlockSpec((SUBLANES, cb), lambda i, j: (jnp.maximum(i * hb - 1, 0), j)),
                  pl.BlockSpec((DN_CONV_WIDTH, cb), lambda i, j: (0, j))],
        out_specs=pl.BlockSpec((tt, cb), lambda i, j: (i, j)),
        out_shape=jax.ShapeDtypeStruct((s, conv_dim), F32),
        compiler_params=_params(("parallel", "parallel")),
        name="gdn_conv_norm",
    )(proj, proj, conv_w)


def _bmm(a, b):
    return jnp.einsum("nij,njk->nik", a.astype(BF16), b.astype(BF16), preferred_element_type=F32)


def _bmm_nt(a, b):
    return jnp.einsum("nid,njd->nij", a.astype(BF16), b.astype(BF16), preferred_element_type=F32)


def _chunk_cumsum(x, chunk, reverse=False):
    rows = x.shape[0]
    pos = lax.broadcasted_iota(I32, x.shape, 0) % chunk
    step = 1
    while step < chunk:
        if reverse:
            shifted = pltpu.roll(x, rows - step, axis=0)
            x = x + jnp.where(pos + step < chunk, shifted, 0.0)
        else:
            shifted = pltpu.roll(x, step, axis=0)
            x = x + jnp.where(pos >= step, shifted, 0.0)
        step *= 2
    return x


def _cat_lanes(a, b):
    return jnp.concatenate([a, b], axis=-1)


def _block_diag(a, b):
    return jnp.concatenate([_cat_lanes(a, jnp.zeros_like(b)), _cat_lanes(jnp.zeros_like(a), b)], axis=-2)


def _unit_lower_inverse_pair(lower_a, lower_b, eye):
    c = lower_a.shape[-1]
    pa, pb = lower_a.astype(BF16), lower_b.astype(BF16)
    tinv = _cat_lanes(eye - lower_a, eye - lower_b)
    span = 2
    while span < c:
        power = jnp.einsum("nij,njk->nik", _cat_lanes(pa, pb), _block_diag(pa, pb), preferred_element_type=F32)
        pa, pb = power[..., :c].astype(BF16), power[..., c:].astype(BF16)
        tinv = tinv + jnp.einsum("nij,njk->nik", tinv.astype(BF16), _block_diag(pa, pb),
                                 preferred_element_type=F32)
        span *= 2
    return tinv[..., :c], tinv[..., c:]


def _delta_kernel(q_ref, k_ref, v_ref, z_ref, ba_ref, alog_ref, dt_ref, nw_ref, o_ref,
                  state_ref, u_ref, wq_ref, ak_ref, egl_ref, osc_ref, *, hb):
    t = pl.program_id(1)
    tb = q_ref.shape[0]
    c = SOLVE_CHUNK
    nc = tb // c
    d = HEAD_DIM

    @pl.when(t == 0)
    def _():
        state_ref[...] = jnp.zeros_like(state_ref)

    ba = ba_ref[...]
    beta_all = _sigmoid(ba)
    g_all = -jnp.exp(alog_ref[...]) * _softplus(ba + dt_ref[...])
    big_g = _chunk_cumsum(g_all, c)
    tail_g = _chunk_cumsum(g_all, c, reverse=True) - g_all
    e_g = jnp.exp(big_g)
    e_tail = jnp.exp(tail_g)
    e_tot = jnp.exp(big_g + tail_g)
    big_g_t = big_g.T

    row = lax.broadcasted_iota(I32, (c, c), 0)
    col = lax.broadcasted_iota(I32, (c, c), 1)
    causal = row >= col
    strict = row > col
    eye = (row == col).astype(F32)[None]

    def head_terms(hh):
        gi = hb + hh
        gcol_b = jnp.broadcast_to(big_g[:, gi:gi + 1], (tb, c))
        decays = []
        for n in range(nc):
            diff = gcol_b[n * c:(n + 1) * c, :] - big_g_t[gi:gi + 1, n * c:(n + 1) * c]
            decays.append(jnp.where(causal, jnp.exp(jnp.where(causal, diff, 0.0)), 0.0))
        return beta_all[:, hh:hh + 1], e_g[:, gi:gi + 1], e_tail[:, gi:gi + 1], jnp.stack(decays, axis=0)

    for p in range(hb // 2):
        heads = (2 * p, 2 * p + 1)
        qh = q_ref[:, p * d:(p + 1) * d]
        kh = k_ref[:, p * d:(p + 1) * d]
        terms = [head_terms(hh) for hh in heads]
        kbs = [kh * tm_[0] for tm_ in terms]
        stacked = jnp.concatenate([kbs[0].reshape(nc, c, d), kbs[1].reshape(nc, c, d), qh.reshape(nc, c, d)],
                                  axis=1)
        prod = _bmm_nt(stacked, kh.reshape(nc, c, d))
        qk = prod[:, 2 * c:]
        lowers = [jnp.where(strict[None], prod[:, i * c:(i + 1) * c] * terms[i][3], 0.0) for i in range(2)]
        tinvs = _unit_lower_inverse_pair(lowers[0], lowers[1], eye)
        packed = []
        for i, hh in enumerate(heads):
            bcol, egcol, etcol, decay = terms[i]
            vh = v_ref[:, hh * d:(hh + 1) * d]
            rhs = _cat_lanes((vh * bcol).reshape(nc, c, d), (kbs[i] * egcol).reshape(nc, c, d))
            sol = _bmm(tinvs[i], rhs)
            u_ref[hh] = sol[..., :d]
            wq_i = jnp.concatenate([sol[..., d:], (qh * egcol).reshape(nc, c, d)], axis=1).astype(BF16)
            kd_t = (kh * etcol).T.astype(BF16)
            kdt3 = jnp.stack([kd_t[:, n * c:(n + 1) * c] for n in range(nc)], axis=0)
            packed.append((wq_i, jnp.concatenate([(qk * decay).astype(BF16), kdt3], axis=1)))
            egl_ref[hh] = jnp.broadcast_to(e_tot[:, hb + hh:hb + hh + 1], (tb, d))
        wq_ref[p] = _cat_lanes(packed[0][0], packed[1][0])
        ak_ref[p] = _cat_lanes(packed[0][1], packed[1][1])

    def chunk_step(n, carry):
        r0 = pl.multiple_of(n * c, c)
        for p in range(hb // 2):
            ha, hb_ = 2 * p, 2 * p + 1
            sa, sb = state_ref[ha], state_ref[hb_]
            ws = jnp.dot(wq_ref[p, n], _block_diag(sa.astype(BF16), sb.astype(BF16)),
                         preferred_element_type=F32)
            vna = u_ref[ha, n] - ws[:c, :d]
            vnb = u_ref[hb_, n] - ws[:c, d:]
            r = jnp.dot(ak_ref[p, n], _block_diag(vna.astype(BF16), vnb.astype(BF16)),
                        preferred_element_type=F32)
            osc_ref[pl.ds(r0, c), ha * d:(ha + 1) * d] = ws[c:, :d] + r[:c, :d]
            osc_ref[pl.ds(r0, c), hb_ * d:(hb_ + 1) * d] = ws[c:, d:] + r[:c, d:]
            state_ref[ha] = sa * egl_ref[ha, pl.ds(r0, 1), :] + r[c:, :d]
            state_ref[hb_] = sb * egl_ref[hb_, pl.ds(r0, 1), :] + r[c:, d:]
        return carry

    lax.fori_loop(0, nc, chunk_step, 0)

    nw = nw_ref[...]
    for hh in range(hb):
        o = osc_ref[:, hh * d:(hh + 1) * d]
        z = z_ref[:, hh * d:(hh + 1) * d]
        ms = jnp.mean(o * o, axis=1, keepdims=True)
        y = o * lax.rsqrt(ms + NORM_EPS) * nw * (z * _sigmoid(z))
        o_ref[:, hh * d:(hh + 1) * d] = y.astype(BF16)


def gdn_delta_rule(qkv, proj, ba_g, alog_g, dt_g, norm_w, n_v_heads, *, hb, tb_pref=512):
    s = qkv.shape[0]
    d = HEAD_DIM
    c = SOLVE_CHUNK
    qk_dim = (n_v_heads // 2) * d
    v_dim = n_v_heads * d
    tb = _tile(s, tb_pref)
    nc = tb // c
    hg = n_v_heads // hb
    qw = (hb // 2) * d
    vw = hb * d
    return pl.pallas_call(
        functools.partial(_delta_kernel, hb=hb),
        grid=(hg, s // tb),
        in_specs=[pl.BlockSpec((tb, qw), lambda g, t: (t, g)),
                  pl.BlockSpec((tb, qw), lambda g, t: (t, qk_dim // qw + g)),
                  pl.BlockSpec((tb, vw), lambda g, t: (t, 2 * qk_dim // vw + g)),
                  pl.BlockSpec((tb, vw), lambda g, t: (t, (2 * qk_dim + v_dim) // vw + g)),
                  pl.BlockSpec((tb, LANES), lambda g, t: (t, g)),
                  pl.BlockSpec((None, 1, LANES), lambda g, t: (g, 0, 0)),
                  pl.BlockSpec((None, 1, LANES), lambda g, t: (g, 0, 0)),
                  pl.BlockSpec((1, d), lambda g, t: (0, 0))],
        out_specs=pl.BlockSpec((tb, vw), lambda g, t: (t, g)),
        out_shape=jax.ShapeDtypeStruct((s, v_dim), BF16),
        scratch_shapes=[pltpu.VMEM((hb, d, d), F32),
                        pltpu.VMEM((hb, nc, c, d), F32),
                        pltpu.VMEM((hb // 2, nc, 2 * c, 2 * d), BF16),
                        pltpu.VMEM((hb // 2, nc, c + d, 2 * c), BF16),
                        pltpu.VMEM((hb, tb, d), F32),
                        pltpu.VMEM((tb, vw), F32)],
        compiler_params=_params(("parallel", "arbitrary")),
        name="gdn_delta_rule",
    )(qkv, qkv, qkv, proj, ba_g, alog_g, dt_g, norm_w.reshape(1, d))


def _split3_bf16(x):
    hi = x.astype(BF16).astype(F32)
    r1 = x - hi
    mid = r1.astype(BF16).astype(F32)
    return hi, mid, r1 - mid


def _fox_pre_kernel(q_ref, k_ref, v_ref, c_ref, qw_ref, kw_ref, qa_ref, ka_ref, vt_ref, *, n_heads):
    d = HEAD_DIM
    tt = q_ref.shape[0]
    c2 = c_ref[...] * LOG2E
    lane = lax.broadcasted_iota(I32, (tt, d), 1)
    qw = qw_ref[...] * (d ** -0.5 * LOG2E)
    kw = kw_ref[...]
    for h in range(n_heads):
        qh = q_ref[:, h * d:(h + 1) * d]
        kh = k_ref[:, h * d:(h + 1) * d]
        qn = qh * lax.rsqrt(jnp.mean(qh * qh, axis=1, keepdims=True) + NORM_EPS) * qw
        kn = kh * lax.rsqrt(jnp.mean(kh * kh, axis=1, keepdims=True) + NORM_EPS) * kw
        hi, mid, lo = _split3_bf16(c2[:, h:h + 1])
        aq = jnp.where(lane == 0, hi, jnp.where(lane == 1, mid, jnp.where(lane == 2, lo,
                                                                          jnp.where(lane < 6, 1.0, 0.0))))
        ak = jnp.where(lane < 3, 1.0, jnp.where(lane == 3, -hi, jnp.where(lane == 4, -mid,
                                                                           jnp.where(lane == 5, -lo, 0.0))))
        qa_ref[:, 2 * h * d:(2 * h + 1) * d] = qn.astype(BF16)
        qa_ref[:, (2 * h + 1) * d:(2 * h + 2) * d] = aq.astype(BF16)
        ka_ref[:, 2 * h * d:(2 * h + 1) * d] = kn.astype(BF16)
        ka_ref[:, (2 * h + 1) * d:(2 * h + 2) * d] = ak.astype(BF16)
    vt_ref[...] = v_ref[...].T.astype(BF16)


def fox_preprocess(proj, c_pad, q_norm_w, k_norm_w, n_heads, *, tt_pref=256):
    s = proj.shape[0]
    d = HEAD_DIM
    dim = n_heads * d
    tt = _tile(s, tt_pref)
    return pl.pallas_call(
        functools.partial(_fox_pre_kernel, n_heads=n_heads),
        grid=(s // tt,),
        in_specs=[pl.BlockSpec((tt, dim), lambda i: (i, 0)),
                  pl.BlockSpec((tt, dim), lambda i: (i, 1)),
                  pl.BlockSpec((tt, dim), lambda i: (i, 2)),
                  pl.BlockSpec((tt, LANES), lambda i: (i, 0)),
                  pl.BlockSpec((1, d), lambda i: (0, 0)),
                  pl.BlockSpec((1, d), lambda i: (0, 0))],
        out_specs=[pl.BlockSpec((tt, 2 * dim), lambda i: (i, 0)),
                   pl.BlockSpec((tt, 2 * dim), lambda i: (i, 0)),
                   pl.BlockSpec((dim, tt), lambda i: (0, i))],
        out_shape=[jax.ShapeDtypeStruct((s, 2 * dim), BF16), jax.ShapeDtypeStruct((s, 2 * dim), BF16),
                   jax.ShapeDtypeStruct((dim, s), BF16)],
        compiler_params=_params(("parallel",)),
        name="fox_qkv_prep",
    )(proj, proj, proj, c_pad, q_norm_w.reshape(1, d), k_norm_w.reshape(1, d))


def _fox_cumgate_kernel(f_ref, b_ref, o_ref, carry_ref):
    i = pl.program_id(0)
    tt = f_ref.shape[0]

    @pl.when(i == 0)
    def _():
        carry_ref[...] = jnp.zeros_like(carry_ref)

    x = f_ref[...] + b_ref[...]
    log_f = jnp.minimum(x, 0.0) - jnp.log(1.0 + jnp.exp(-jnp.abs(x)))
    csum = _chunk_cumsum(log_f, tt) + carry_ref[...]
    o_ref[...] = csum
    carry_ref[...] = csum[tt - 1:tt, :]


def fox_cumulative_gate(f_logits, b_row, *, tt_pref=512):
    s = f_logits.shape[0]
    tt = _tile(s, tt_pref)
    return pl.pallas_call(
        _fox_cumgate_kernel,
        grid=(s // tt,),
        in_specs=[pl.BlockSpec((tt, LANES), lambda i: (i, 0)),
                  pl.BlockSpec((1, LANES), lambda i: (0, 0))],
        out_specs=pl.BlockSpec((tt, LANES), lambda i: (i, 0)),
        out_shape=jax.ShapeDtypeStruct((s, LANES), F32),
        scratch_shapes=[pltpu.VMEM((1, LANES), F32)],
        compiler_params=_params(("arbitrary",)),
        name="fox_cumulative_gate",
    )(f_logits, b_row)


def _fox_attn_kernel(q_ref, k_ref, vt_ref, gate_ref, o_ref, m_ref, l_ref, acc_ref):
    qi = pl.program_id(1)
    tq = q_ref.shape[0]
    tk = tq
    q = q_ref[...]

    m_ref[...] = jnp.full_like(m_ref, NEG_BIG)
    l_ref[...] = jnp.zeros_like(l_ref)
    acc_ref[...] = jnp.zeros_like(acc_ref)

    def block(kj, masked):
        k0 = pl.multiple_of(kj * tk, tk)
        kb = k_ref[pl.ds(k0, tk), :]
        st = lax.dot_general(kb, q, (((1,), (1,)), ((), ())), preferred_element_type=F32)
        if masked:
            kpos = lax.broadcasted_iota(I32, (tk, tq), 0)
            qpos = lax.broadcasted_iota(I32, (tk, tq), 1)
            st = jnp.where(kpos <= qpos, st, -jnp.inf)
        m_old = m_ref[...]
        m_new = jnp.maximum(m_old, jnp.max(st, axis=0, keepdims=True))
        alpha = jnp.exp2(m_old - m_new)
        p = jnp.exp2(st - m_new)
        l_ref[...] = alpha * l_ref[...] + jnp.sum(p, axis=0, keepdims=True)
        vb = vt_ref[:, pl.ds(k0, tk)]
        acc_ref[...] = alpha * acc_ref[...] + jnp.dot(vb, p.astype(BF16), preferred_element_type=F32)
        m_ref[...] = m_new

    def pair_body(i, carry):
        block(2 * i, False)
        block(2 * i + 1, False)
        return carry

    lax.fori_loop(0, qi // 2, pair_body, 0)

    @pl.when(qi % 2 == 1)
    def _():
        block(qi - 1, False)

    block(qi, True)

    o_t = acc_ref[...] / l_ref[...]
    o_ref[...] = (o_t.T * _sigmoid(gate_ref[...])).astype(BF16)


def fox_attention(qa, ka, v_t, proj, n_heads, *, tq_pref=512):
    s = qa.shape[0]
    d = HEAD_DIM
    tq = _tile(s, tq_pref)
    return pl.pallas_call(
        _fox_attn_kernel,
        grid=(n_heads, s // tq),
        in_specs=[pl.BlockSpec((tq, 2 * d), lambda h, i: (i, h)),
                  pl.BlockSpec((s, 2 * d), lambda h, i: (0, h)),
                  pl.BlockSpec((d, s), lambda h, i: (h, 0)),
                  pl.BlockSpec((tq, d), lambda h, i: (i, 3 * n_heads + h))],
        out_specs=pl.BlockSpec((tq, d), lambda h, i: (i, h)),
        out_shape=jax.ShapeDtypeStruct((s, n_heads * d), BF16),
        scratch_shapes=[pltpu.VMEM((1, tq), F32), pltpu.VMEM((1, tq), F32), pltpu.VMEM((d, tq), F32)],
        compiler_params=_params(("parallel", "arbitrary")),
        name="fox_attention",
    )(qa, ka, v_t, proj)


def _split_bf16(x):
    hi = x.astype(BF16)
    lo = (x - hi.astype(F32)).astype(BF16)
    return hi, lo


def _first_argmax(vals, mask, lane_f):
    v = jnp.where(mask, vals, NEG_BIG)
    vmax = jnp.max(v, axis=1, keepdims=True)
    hit = jnp.logical_and(mask, v == vmax)
    idx = jnp.min(jnp.where(hit, lane_f, float(LANES)), axis=1, keepdims=True)
    return vmax, idx


def _router_kernel(x_ref, w_ref, b_ref, ids_ref, wts_ref, rank_ref, cnt_ref, carry_ref, *, n_groups, n_exp):
    i = pl.program_id(0)
    tm = x_ref.shape[0]

    @pl.when(i == 0)
    def _():
        carry_ref[...] = jnp.zeros_like(carry_ref)

    xh, xl = _split_bf16(x_ref[...])
    wh, wl = _split_bf16(w_ref[...])
    lg = (jnp.dot(xh, wh, preferred_element_type=F32) + jnp.dot(xh, wl, preferred_element_type=F32)
          + jnp.dot(xl, wh, preferred_element_type=F32)) + b_ref[...]
    lane = lax.broadcasted_iota(I32, lg.shape, 1)
    lane_f = lane.astype(F32)

    gmask = lane < n_groups
    gmax, gidx = _first_argmax(lg, gmask, lane_f)
    gsum = jnp.sum(jnp.where(gmask, jnp.exp(jnp.where(gmask, lg, NEG_BIG) - gmax), 0.0), axis=1, keepdims=True)
    p_group = 1.0 / gsum

    e_lo = float(n_groups) + gidx * float(n_exp)
    emask = jnp.logical_and(lane_f >= e_lo, lane_f < e_lo + float(n_exp))
    t1, i1 = _first_argmax(lg, emask, lane_f)
    emask2 = jnp.logical_and(emask, lane_f != i1)
    t2, i2 = _first_argmax(lg, emask2, lane_f)
    e21 = jnp.exp(t2 - t1)
    p1 = p_group / (1.0 + e21)
    p2 = p_group * e21 / (1.0 + e21)
    id1 = i1 - float(n_groups)
    id2 = i2 - float(n_groups)

    oh = jnp.where(lane_f == id1, 1.0, 0.0) + jnp.where(lane_f == id2, 1.0, 0.0)
    r = lax.broadcasted_iota(I32, (tm, tm), 0)
    cc = lax.broadcasted_iota(I32, (tm, tm), 1)
    tril = jnp.where(r > cc, 1.0, 0.0).astype(BF16)
    before = jnp.dot(tril, oh.astype(BF16), preferred_element_type=F32) + carry_ref[...]
    rank1 = jnp.sum(jnp.where(lane_f == id1, before, 0.0), axis=1, keepdims=True)
    rank2 = jnp.sum(jnp.where(lane_f == id2, before, 0.0), axis=1, keepdims=True)
    new_carry = carry_ref[...] + jnp.sum(oh, axis=0, keepdims=True)
    carry_ref[...] = new_carry
    cnt_ref[...] = jnp.broadcast_to(new_carry, cnt_ref.shape).astype(I32)

    first = lane == 0
    second = lane == 1
    ids_ref[...] = jnp.where(first, id1, jnp.where(second, id2, 0.0)).astype(I32)
    wts_ref[...] = jnp.where(first, p1, jnp.where(second, p2, 0.0))
    rank_ref[...] = jnp.where(first, rank1, jnp.where(second, rank2, 0.0)).astype(I32)


def moe_router(x, w_pad, b_pad, n_groups, n_exp, *, tm_pref=512):
    s, dm = x.shape
    tm = _tile(s, tm_pref)
    return pl.pallas_call(
        functools.partial(_router_kernel, n_groups=n_groups, n_exp=n_exp),
        grid=(s // tm,),
        in_specs=[pl.BlockSpec((tm, dm), lambda i: (i, 0)),
                  pl.BlockSpec((dm, LANES), lambda i: (0, 0)),
                  pl.BlockSpec((1, LANES), lambda i: (0, 0))],
        out_specs=[pl.BlockSpec((tm, LANES), lambda i: (i, 0)),
                   pl.BlockSpec((tm, LANES), lambda i: (i, 0)),
                   pl.BlockSpec((tm, LANES), lambda i: (i, 0)),
                   pl.BlockSpec((SUBLANES, LANES), lambda i: (0, 0))],
        out_shape=[jax.ShapeDtypeStruct((s, LANES), I32), jax.ShapeDtypeStruct((s, LANES), F32),
                   jax.ShapeDtypeStruct((s, LANES), I32), jax.ShapeDtypeStruct((SUBLANES, LANES), I32)],
        scratch_shapes=[pltpu.VMEM((1, LANES), F32)],
        compiler_params=_params(("arbitrary",)),
        name="moe_router",
    )(x, w_pad, b_pad)


def _start_row_gather(src_hbm, dst, sem, idx_ref, base):
    for r in range(dst.shape[0]):
        tok = idx_ref[base + r]
        pltpu.make_async_copy(src_hbm.at[pl.ds(tok, 1), :], dst.at[pl.ds(r, 1), :], sem).start()


def _wait_row_gather(dst, sem):
    pltpu.make_async_copy(dst, dst, sem).wait()


def _moe_ffn_kernel(te_ref, nu_ref, src_ref, x_hbm, wg_ref, wu_ref, wd_ref, o_ref,
                    xbuf, sem, wgb, wub, wdb):
    i = pl.program_id(0)
    tm = o_ref.shape[0]
    n_used = nu_ref[0]

    @pl.when(i == 0)
    def _():
        _start_row_gather(x_hbm, xbuf.at[0], sem.at[0], src_ref, 0)

    def used_tile(slot):
        is_new = jnp.logical_or(i == 0, te_ref[i] != te_ref[jnp.maximum(i - 1, 0)])

        @pl.when(is_new)
        def _():
            wgb[...] = wg_ref[...].astype(BF16)
            wub[...] = wu_ref[...].astype(BF16)
            wdb[...] = wd_ref[...].astype(BF16)

        _wait_row_gather(xbuf.at[slot], sem.at[slot])
        _start_row_gather(x_hbm, xbuf.at[1 - slot], sem.at[1 - slot], src_ref, (i + 1) * tm)
        x = xbuf[slot].astype(BF16)
        gate = jnp.dot(x, wgb[...], preferred_element_type=F32)
        up = jnp.dot(x, wub[...], preferred_element_type=F32)
        hmid = (gate * _sigmoid(gate) * up).astype(BF16)
        o_ref[...] = jnp.dot(hmid, wdb[...], preferred_element_type=F32)

    for slot in range(2):
        pl.when(jnp.logical_and(i < n_used, i % 2 == slot))(functools.partial(used_tile, slot))

    @pl.when(i >= n_used)
    def _():
        for slot in range(2):
            @pl.when(jnp.logical_and(i == n_used, i % 2 == slot))
            def _():
                _wait_row_gather(xbuf.at[slot], sem.at[slot])

        o_ref[...] = jnp.zeros_like(o_ref)


def moe_expert_ffn(x, w_gate, w_up, w_down, layer, tile_expert, n_used, src_tok, tm):
    s, dm = x.shape
    n_exp_total = MOE_GROUPS * MOE_EXPERTS_PER_GROUP
    fe = w_gate.shape[-1]
    n_tiles = tile_expert.shape[0]
    wg = w_gate.reshape(-1, dm, fe)
    wu = w_up.reshape(-1, dm, fe)
    wd = w_down.reshape(-1, fe, dm)
    base = layer * n_exp_total
    grid_spec = pltpu.PrefetchScalarGridSpec(
        num_scalar_prefetch=3,
        grid=(n_tiles,),
        in_specs=[pl.BlockSpec(memory_space=pl.ANY),
                  pl.BlockSpec((None, dm, fe), lambda i, te, nu, src: (base + te[i], 0, 0)),
                  pl.BlockSpec((None, dm, fe), lambda i, te, nu, src: (base + te[i], 0, 0)),
                  pl.BlockSpec((None, fe, dm), lambda i, te, nu, src: (base + te[i], 0, 0))],
        out_specs=pl.BlockSpec((tm, dm), lambda i, te, nu, src: (i, 0)),
        scratch_shapes=[pltpu.VMEM((2, tm, dm), F32),
                        pltpu.SemaphoreType.DMA((2,)),
                        pltpu.VMEM((dm, fe), BF16),
                        pltpu.VMEM((dm, fe), BF16),
                        pltpu.VMEM((fe, dm), BF16)])
    return pl.pallas_call(
        _moe_ffn_kernel,
        grid_spec=grid_spec,
        out_shape=jax.ShapeDtypeStruct((n_tiles * tm, dm), F32),
        compiler_params=_params(("arbitrary",)),
        name="moe_expert_ffn",
    )(tile_expert, n_used, src_tok, x, wg, wu, wd)


def _moe_combine_kernel(pos_ref, y_hbm, x_ref, wts_ref, g_ref, b_ref, o_ref, ob_ref, ybuf, sem, *, alpha):
    i = pl.program_id(0)
    n = pl.num_programs(0)
    tm = x_ref.shape[0]

    def gather(tile, sl):
        for kk in range(MOE_TOP_K):
            _start_row_gather(y_hbm, ybuf.at[sl, kk], sem.at[sl], pos_ref, (kk * n + tile) * tm)

    @pl.when(i == 0)
    def _():
        gather(0, 0)

    def tile_body(slot, prefetch):
        _wait_row_gather(ybuf.at[slot], sem.at[slot])
        if prefetch:
            gather(i + 1, 1 - slot)
        wts = wts_ref[...]
        mix = wts[:, 0:1] * ybuf[slot, 0] + wts[:, 1:2] * ybuf[slot, 1]
        y = alpha * x_ref[...] + mix
        out = _layernorm_rows(y, g_ref[...], b_ref[...])
        o_ref[...] = out
        ob_ref[...] = out.astype(BF16)

    for slot in range(2):
        pl.when(jnp.logical_and(i + 1 < n, i % 2 == slot))(functools.partial(tile_body, slot, True))
        pl.when(jnp.logical_and(i + 1 == n, i % 2 == slot))(functools.partial(tile_body, slot, False))


def moe_combine_layernorm(y_sorted, pos_km, x, wts, g, b, alpha, *, tm_pref=256):
    s, dm = x.shape
    tm = _tile(s, tm_pref)
    grid_spec = pltpu.PrefetchScalarGridSpec(
        num_scalar_prefetch=1,
        grid=(s // tm,),
        in_specs=[pl.BlockSpec(memory_space=pl.ANY),
                  pl.BlockSpec((tm, dm), lambda i, pos: (i, 0)),
                  pl.BlockSpec((tm, LANES), lambda i, pos: (i, 0)),
                  pl.BlockSpec((1, dm), lambda i, pos: (0, 0)),
                  pl.BlockSpec((1, dm), lambda i, pos: (0, 0))],
        out_specs=[pl.BlockSpec((tm, dm), lambda i, pos: (i, 0)),
                   pl.BlockSpec((tm, dm), lambda i, pos: (i, 0))],
        scratch_shapes=[pltpu.VMEM((2, MOE_TOP_K, tm, dm), F32),
                        pltpu.SemaphoreType.DMA((2,))])
    return pl.pallas_call(
        functools.partial(_moe_combine_kernel, alpha=alpha),
        grid_spec=grid_spec,
        out_shape=[jax.ShapeDtypeStruct((s, dm), F32), jax.ShapeDtypeStruct((s, dm), BF16)],
        compiler_params=_params(("arbitrary",)),
        name="moe_combine_ln",
    )(pos_km, y_sorted, x, wts, g.reshape(1, dm), b.reshape(1, dm))


def _pad_cols(w, width=LANES):
    return jnp.pad(w, ((0, 0), (0, width - w.shape[1])))


def _moe_layer(x, x_bf16, layer, w_rg, b_rg, w_re, b_re, w_gate, w_up, w_down, ln_g, ln_b, alpha, *, tm=256):
    del x_bf16
    s, dm = x.shape
    n_groups, _, n_exp = w_re.shape
    n_total = n_groups * n_exp
    w_r = _pad_cols(jnp.concatenate([w_rg, jnp.transpose(w_re, (1, 0, 2)).reshape(dm, n_total)], axis=1))
    b_r = _pad_cols(jnp.concatenate([b_rg, b_re.reshape(n_total)])[None, :])
    ids, wts, ranks, counts = moe_router(x, w_r, b_r, n_groups, n_exp)

    n_tiles = (s * MOE_TOP_K) // tm + n_total + 1
    counts = counts[0, :n_total]
    padded = ((counts + tm - 1) // tm) * tm
    ends = jnp.cumsum(padded)
    starts = ends - padded
    ids2 = ids[:, :MOE_TOP_K]
    pos = starts[ids2] + ranks[:, :MOE_TOP_K]
    tok = jnp.broadcast_to(jnp.arange(s, dtype=I32)[:, None], (s, MOE_TOP_K))
    src_tok = jnp.zeros((n_tiles * tm,), I32).at[pos.reshape(-1)].set(tok.reshape(-1))
    n_used = (ends[-1] // tm).astype(I32)
    tile_start = jnp.arange(n_tiles, dtype=I32) * tm
    tile_expert = jnp.minimum(jnp.searchsorted(ends, tile_start, side="right"), n_total - 1).astype(I32)
    last_expert = tile_expert[jnp.maximum(n_used - 1, 0)]
    tile_expert = jnp.where(jnp.arange(n_tiles) < n_used, tile_expert, last_expert)

    y_sorted = moe_expert_ffn(x, w_gate, w_up, w_down, layer, tile_expert, n_used.reshape(1), src_tok, tm)
    pos_km = jnp.transpose(pos.reshape(s // tm, tm, MOE_TOP_K), (2, 0, 1)).reshape(-1).astype(I32)
    return moe_combine_layernorm(y_sorted, pos_km, x, wts, ln_g, ln_b, alpha, tm_pref=tm)


def kernel(x, dn_w_in, dn_conv_w, dn_A_log, dn_dt_bias, dn_norm_w, dn_w_out, fox_w_in, fox_b_f, fox_q_norm_w,
           fox_k_norm_w, fox_w_out, ln1_g, ln1_b, ln2_g, ln2_b, moe_w_rg, moe_b_rg, moe_w_re, moe_b_re,
           moe_w_gate, moe_w_up, moe_w_down):
    batch, s, dm = x.shape
    assert batch == 1
    depth = ln1_g.shape[0]
    alpha = (2 * depth) ** 0.25
    x2d = x.reshape(s, dm)
    xb = x2d.astype(BF16)
    d = HEAD_DIM
    hb = 8

    for i in range(depth):
        j = i // 2
        if i % 2 == 0:
            n_v_heads = dn_A_log.shape[1]
            qk_dim = (n_v_heads // 2) * d
            v_dim = n_v_heads * d
            conv_dim = 2 * qk_dim + v_dim
            w_in = dn_w_in[j]
            proj = matmul(xb, w_in, conv_dim + v_dim)
            hg = n_v_heads // hb
            w_b = w_in[:, conv_dim + v_dim:conv_dim + v_dim + n_v_heads].reshape(dm, hg, hb)
            w_a = w_in[:, conv_dim + v_dim + n_v_heads:].reshape(dm, hg, hb)
            w_ba = jnp.pad(jnp.concatenate([w_b, w_a], axis=2), ((0, 0), (0, 0), (0, LANES - 2 * hb)))
            ba_g = matmul(xb, w_ba.reshape(dm, hg * LANES), hg * LANES)
            zeros = jnp.zeros((hg, hb), F32)
            alog_g = jnp.pad(jnp.concatenate([zeros, dn_A_log[j].reshape(hg, hb)], axis=1),
                             ((0, 0), (0, LANES - 2 * hb))).reshape(hg, 1, LANES)
            dt_g = jnp.pad(jnp.concatenate([zeros, dn_dt_bias[j].reshape(hg, hb)], axis=1),
                           ((0, 0), (0, LANES - 2 * hb))).reshape(hg, 1, LANES)
            qkv = gdn_preprocess(proj, dn_conv_w[j], qk_dim)
            mixed = gdn_delta_rule(qkv, proj, ba_g, alog_g, dt_g, dn_norm_w[j], n_v_heads, hb=hb)
            w_out = dn_w_out[j]
        else:
            n_heads = fox_b_f.shape[1]
            dim = n_heads * d
            w_in = fox_w_in[j]
            proj = matmul(xb, w_in, 4 * dim)
            f_logits = matmul(xb, _pad_cols(w_in[:, 4 * dim:]), LANES, tn_pref=LANES)
            c_pad = fox_cumulative_gate(f_logits, _pad_cols(fox_b_f[j][None, :]))
            qa, ka, v_t = fox_preprocess(proj, c_pad, fox_q_norm_w[j], fox_k_norm_w[j], n_heads)
            mixed = fox_attention(qa, ka, v_t, proj, n_heads)
            w_out = fox_w_out[j]
        x2d, xb = matmul_residual_layernorm(mixed, w_out, x2d, ln1_g[i], ln1_b[i], alpha)
        x2d, xb = _moe_layer(x2d, xb, i, moe_w_rg[i], moe_b_rg[i], moe_w_re[i], moe_b_re[i],
                             moe_w_gate, moe_w_up, moe_w_down, ln2_g[i], ln2_b[i], alpha)
    return x2d.reshape(batch, s, dm)
```

```python
import functools
import math

import jax
import jax.numpy as jnp
from jax import lax
from jax.experimental import pallas as pl
from jax.experimental.pallas import tpu as pltpu

F32 = jnp.float32
BF16 = jnp.bfloat16
I32 = jnp.int32

LANES = 128
SUBLANES = 8
V7X_VMEM_BYTES = 64 * 1024 * 1024
VMEM_LIMIT = V7X_VMEM_BYTES - 8 * 1024 * 1024

HEAD_DIM = 128
SOLVE_CHUNK = 128
DN_CONV_WIDTH = 4
MOE_GROUPS = 4
MOE_EXPERTS_PER_GROUP = 8
MOE_TOP_K = 2
NORM_EPS = 1e-6
NEG_BIG = -1e30
LOG2E = 1.4426950408889634


def _params(sem, vmem=VMEM_LIMIT):
    return pltpu.CompilerParams(dimension_semantics=sem, vmem_limit_bytes=vmem)


def _tile(dim, pref):
    t = min(dim, pref)
    assert dim % t == 0, (dim, pref)
    return t


def _softplus(x):
    return jnp.maximum(x, 0.0) + jnp.log(1.0 + jnp.exp(-jnp.abs(x)))


def _sigmoid(x):
    return 1.0 / (1.0 + jnp.exp(-x))


def _mm_kernel(a_ref, wt_ref, o_ref):
    o_ref[...] = lax.dot_general(a_ref[...], wt_ref[...].astype(BF16), (((1,), (1,)), ((), ())),
                                 preferred_element_type=F32)


def matmul(a, w_t, n_out, *, tm_pref=2048, tn_pref=512):
    m, k = a.shape
    tm = _tile(m, tm_pref)
    tn = _tile(n_out, tn_pref)
    return pl.pallas_call(
        _mm_kernel,
        grid=(m // tm, n_out // tn),
        in_specs=[pl.BlockSpec((tm, k), lambda i, j: (i, 0)),
                  pl.BlockSpec((tn, k), lambda i, j: (j, 0))],
        out_specs=pl.BlockSpec((tm, tn), lambda i, j: (i, j)),
        out_shape=jax.ShapeDtypeStruct((m, n_out), F32),
        compiler_params=_params(("parallel", "parallel")),
        name="proj_matmul",
    )(a, w_t)


def _layernorm_rows(y, g, b):
    mu = jnp.mean(y, axis=1, keepdims=True)
    yc = y - mu
    var = jnp.mean(yc * yc, axis=1, keepdims=True)
    return yc * lax.rsqrt(var + NORM_EPS) * g + b


def _mm_res_ln_kernel(a_ref, w_ref, res_ref, g_ref, b_ref, o_ref, ob_ref, *, alpha):
    y = alpha * res_ref[...] + jnp.dot(a_ref[...], w_ref[...], preferred_element_type=F32)
    out = _layernorm_rows(y, g_ref[...], b_ref[...])
    o_ref[...] = out
    ob_ref[...] = out.astype(BF16)


def matmul_residual_layernorm(a, w_bf16, res, g, b, alpha, *, tm_pref=512):
    m, k = a.shape
    n = w_bf16.shape[1]
    tm = _tile(m, tm_pref)
    return pl.pallas_call(
        functools.partial(_mm_res_ln_kernel, alpha=alpha),
        grid=(m // tm,),
        in_specs=[pl.BlockSpec((tm, k), lambda i: (i, 0)),
                  pl.BlockSpec((k, n), lambda i: (0, 0), pipeline_mode=pl.Buffered(1)),
                  pl.BlockSpec((tm, n), lambda i: (i, 0)),
                  pl.BlockSpec((1, n), lambda i: (0, 0)),
                  pl.BlockSpec((1, n), lambda i: (0, 0))],
        out_specs=[pl.BlockSpec((tm, n), lambda i: (i, 0)),
                   pl.BlockSpec((tm, n), lambda i: (i, 0))],
        out_shape=[jax.ShapeDtypeStruct((m, n), F32), jax.ShapeDtypeStruct((m, n), BF16)],
        compiler_params=_params(("parallel",)),
        name="out_proj_res_ln",
    )(a, w_bf16, res, g.reshape(1, n), b.reshape(1, n))


def _gdn_pre_kernel(cur_ref, halo_ref, w_ref, o_ref, ext_ref, *, n_qk_blocks, q_scale):
    i = pl.program_id(0)
    j = pl.program_id(1)
    tt, cb = cur_ref.shape
    keep = (i > 0).astype(F32)
    ext_ref[0:SUBLANES, :] = halo_ref[...] * keep
    ext_ref[SUBLANES:SUBLANES + tt, :] = cur_ref[...]
    w = w_ref[...]
    acc = cur_ref[...] * w[DN_CONV_WIDTH - 1:DN_CONV_WIDTH]
    for kk in range(DN_CONV_WIDTH - 1):
        off = SUBLANES - (DN_CONV_WIDTH - 1) + kk
        acc = acc + ext_ref[off:off + tt, :] * w[kk:kk + 1]
    y = acc * _sigmoid(acc)

    def write_normed(scale):
        for h in range(cb // HEAD_DIM):
            yh = y[:, h * HEAD_DIM:(h + 1) * HEAD_DIM]
            r = lax.rsqrt(jnp.sum(yh * yh, axis=1, keepdims=True) + NORM_EPS)
            o_ref[:, h * HEAD_DIM:(h + 1) * HEAD_DIM] = yh * (r * scale if scale != 1.0 else r)

    @pl.when(j < n_qk_blocks)
    def _():
        write_normed(q_scale)

    @pl.when(jnp.logical_and(j >= n_qk_blocks, j < 2 * n_qk_blocks))
    def _():
        write_normed(1.0)

    @pl.when(j >= 2 * n_qk_blocks)
    def _():
        o_ref[...] = y


def gdn_preprocess(proj, conv_w, qk_dim, *, tt_pref=512, cb_pref=512):
    s = proj.shape[0]
    conv_dim = conv_w.shape[1]
    tt = _tile(s, tt_pref)
    cb = _tile(qk_dim, cb_pref)
    assert conv_dim % cb == 0 and tt % SUBLANES == 0
    hb = tt // SUBLANES
    return pl.pallas_call(
        functools.partial(_gdn_pre_kernel, n_qk_blocks=qk_dim // cb, q_scale=HEAD_DIM ** -0.5),
        grid=(s // tt, conv_dim // cb),
        in_specs=[pl.BlockSpec((tt, cb), lambda i, j: (i, j)),
                  pl.BlockSpec((SUBLANES, cb), lambda i, j: (jnp.maximum(i * hb - 1, 0), j)),
                  pl.BlockSpec((DN_CONV_WIDTH, cb), lambda i, j: (0, j))],
        out_specs=pl.BlockSpec((tt, cb), lambda i, j: (i, j)),
        out_shape=jax.ShapeDtypeStruct((s, conv_dim), F32),
        scratch_shapes=[pltpu.VMEM((tt + SUBLANES, cb), F32)],
        compiler_params=_params(("parallel", "parallel")),
        name="gdn_conv_norm",
    )(proj, proj, conv_w)


def _bmm(a, b):
    return jnp.einsum("nij,njk->nik", a.astype(BF16), b.astype(BF16), preferred_element_type=F32)


def _bmm_nt(a, b):
    return jnp.einsum("nid,njd->nij", a.astype(BF16), b.astype(BF16), preferred_element_type=F32)


def _chunk_cumsum(x, chunk, reverse=False):
    rows = x.shape[0]
    pos = lax.broadcasted_iota(I32, x.shape, 0) % chunk
    step = 1
    while step < chunk:
        if reverse:
            shifted = pltpu.roll(x, rows - step, axis=0)
            x = x + jnp.where(pos + step < chunk, shifted, 0.0)
        else:
            shifted = pltpu.roll(x, step, axis=0)
            x = x + jnp.where(pos >= step, shifted, 0.0)
        step *= 2
    return x


def _cat_lanes(a, b):
    return jnp.concatenate([a, b], axis=-1)


def _block_diag(a, b):
    return jnp.concatenate([_cat_lanes(a, jnp.zeros_like(b)), _cat_lanes(jnp.zeros_like(a), b)], axis=-2)


def _unit_lower_inverse_pair(lower_a, lower_b, eye):
    c = lower_a.shape[-1]
    pa, pb = lower_a.astype(BF16), lower_b.astype(BF16)
    tinv = _cat_lanes(eye - lower_a, eye - lower_b)
    span = 2
    while span < c:
        power = jnp.einsum("nij,njk->nik", _cat_lanes(pa, pb), _block_diag(pa, pb), preferred_element_type=F32)
        pa, pb = power[..., :c].astype(BF16), power[..., c:].astype(BF16)
        tinv = tinv + jnp.einsum("nij,njk->nik", tinv.astype(BF16), _block_diag(pa, pb),
                                 preferred_element_type=F32)
        span *= 2
    return tinv[..., :c], tinv[..., c:]


def _delta_kernel(q_ref, k_ref, v_ref, z_ref, ba_ref, alog_ref, dt_ref, nw_ref, o_ref,
                  state_ref, u_ref, wq_ref, ak_ref, egl_ref, osc_ref, *, hb):
    t = pl.program_id(1)
    tb = q_ref.shape[0]
    c = SOLVE_CHUNK
    nc = tb // c
    d = HEAD_DIM

    @pl.when(t == 0)
    def _():
        state_ref[...] = jnp.zeros_like(state_ref)

    ba = ba_ref[...]
    beta_all = _sigmoid(ba)
    g_all = -jnp.exp(alog_ref[...]) * _softplus(ba + dt_ref[...])
    big_g = _chunk_cumsum(g_all, c)
    tail_g = _chunk_cumsum(g_all, c, reverse=True) - g_all
    e_g = jnp.exp(big_g)
    e_tail = jnp.exp(tail_g)
    e_tot = jnp.exp(big_g + tail_g)
    big_g_t = big_g.T

    row = lax.broadcasted_iota(I32, (c, c), 0)
    col = lax.broadcasted_iota(I32, (c, c), 1)
    causal = row >= col
    strict = row > col
    eye = (row == col).astype(F32)[None]

    def head_terms(hh):
        gi = hb + hh
        gcol_b = jnp.broadcast_to(big_g[:, gi:gi + 1], (tb, c))
        decays = []
        for n in range(nc):
            diff = gcol_b[n * c:(n + 1) * c, :] - big_g_t[gi:gi + 1, n * c:(n + 1) * c]
            decays.append(jnp.where(causal, jnp.exp(jnp.where(causal, diff, 0.0)), 0.0))
        return beta_all[:, hh:hh + 1], e_g[:, gi:gi + 1], e_tail[:, gi:gi + 1], jnp.stack(decays, axis=0)

    for p in range(hb // 2):
        heads = (2 * p, 2 * p + 1)
        qh = q_ref[:, p * d:(p + 1) * d]
        kh = k_ref[:, p * d:(p + 1) * d]
        terms = [head_terms(hh) for hh in heads]
        kbs = [kh * tm_[0] for tm_ in terms]
        stacked = jnp.concatenate([kbs[0].reshape(nc, c, d), kbs[1].reshape(nc, c, d), qh.reshape(nc, c, d)],
                                  axis=1)
        prod = _bmm_nt(stacked, kh.reshape(nc, c, d))
        qk = prod[:, 2 * c:]
        lowers = [jnp.where(strict[None], prod[:, i * c:(i + 1) * c] * terms[i][3], 0.0) for i in range(2)]
        tinvs = _unit_lower_inverse_pair(lowers[0], lowers[1], eye)
        packed = []
        for i, hh in enumerate(heads):
            bcol, egcol, etcol, decay = terms[i]
            vh = v_ref[:, hh * d:(hh + 1) * d]
            rhs = _cat_lanes((vh * bcol).reshape(nc, c, d), (kbs[i] * egcol).reshape(nc, c, d))
            sol = _bmm(tinvs[i], rhs)
            u_ref[hh] = sol[..., :d]
            wq_i = jnp.concatenate([sol[..., d:], (qh * egcol).reshape(nc, c, d)], axis=1).astype(BF16)
            kd_t = (kh * etcol).T.astype(BF16)
            kdt3 = jnp.stack([kd_t[:, n * c:(n + 1) * c] for n in range(nc)], axis=0)
            packed.append((wq_i, jnp.concatenate([(qk * decay).astype(BF16), kdt3], axis=1)))
            egl_ref[hh] = jnp.broadcast_to(e_tot[:, hb + hh:hb + hh + 1], (tb, d))
        wq_ref[p] = _cat_lanes(packed[0][0], packed[1][0])
        ak_ref[p] = _cat_lanes(packed[0][1], packed[1][1])

    def chunk_step(n, carry):
        r0 = pl.multiple_of(n * c, c)
        for p in range(hb // 2):
            ha, hb_ = 2 * p, 2 * p + 1
            sa, sb = state_ref[ha], state_ref[hb_]
            ws = jnp.dot(wq_ref[p, n], _block_diag(sa.astype(BF16), sb.astype(BF16)),
                         preferred_element_type=F32)
            vna = u_ref[ha, n] - ws[:c, :d]
            vnb = u_ref[hb_, n] - ws[:c, d:]
            r = jnp.dot(ak_ref[p, n], _block_diag(vna.astype(BF16), vnb.astype(BF16)),
                        preferred_element_type=F32)
            osc_ref[pl.ds(r0, c), ha * d:(ha + 1) * d] = ws[c:, :d] + r[:c, :d]
            osc_ref[pl.ds(r0, c), hb_ * d:(hb_ + 1) * d] = ws[c:, d:] + r[:c, d:]
            state_ref[ha] = sa * egl_ref[ha, pl.ds(r0, 1), :] + r[c:, :d]
            state_ref[hb_] = sb * egl_ref[hb_, pl.ds(r0, 1), :] + r[c:, d:]
        return carry

    lax.fori_loop(0, nc, chunk_step, 0)

    nw = nw_ref[...]
    for hh in range(hb):
        o = osc_ref[:, hh * d:(hh + 1) * d]
        z = z_ref[:, hh * d:(hh + 1) * d]
        ms = jnp.mean(o * o, axis=1, keepdims=True)
        y = o * lax.rsqrt(ms + NORM_EPS) * nw * (z * _sigmoid(z))
        o_ref[:, hh * d:(hh + 1) * d] = y.astype(BF16)


def gdn_delta_rule(qkv, proj, ba_g, alog_g, dt_g, norm_w, n_v_heads, *, hb, tb_pref=512):
    s = qkv.shape[0]
    d = HEAD_DIM
    c = SOLVE_CHUNK
    qk_dim = (n_v_heads // 2) * d
    v_dim = n_v_heads * d
    tb = _tile(s, tb_pref)
    nc = tb // c
    hg = n_v_heads // hb
    qw = (hb // 2) * d
    vw = hb * d
    return pl.pallas_call(
        functools.partial(_delta_kernel, hb=hb),
        grid=(hg, s // tb),
        in_specs=[pl.BlockSpec((tb, qw), lambda g, t: (t, g)),
                  pl.BlockSpec((tb, qw), lambda g, t: (t, qk_dim // qw + g)),
                  pl.BlockSpec((tb, vw), lambda g, t: (t, 2 * qk_dim // vw + g)),
                  pl.BlockSpec((tb, vw), lambda g, t: (t, (2 * qk_dim + v_dim) // vw + g)),
                  pl.BlockSpec((tb, LANES), lambda g, t: (t, g)),
                  pl.BlockSpec((None, 1, LANES), lambda g, t: (g, 0, 0)),
                  pl.BlockSpec((None, 1, LANES), lambda g, t: (g, 0, 0)),
                  pl.BlockSpec((1, d), lambda g, t: (0, 0))],
        out_specs=pl.BlockSpec((tb, vw), lambda g, t: (t, g)),
        out_shape=jax.ShapeDtypeStruct((s, v_dim), BF16),
        scratch_shapes=[pltpu.VMEM((hb, d, d), F32),
                        pltpu.VMEM((hb, nc, c, d), F32),
                        pltpu.VMEM((hb // 2, nc, 2 * c, 2 * d), BF16),
                        pltpu.VMEM((hb // 2, nc, c + d, 2 * c), BF16),
                        pltpu.VMEM((hb, tb, d), F32),
                        pltpu.VMEM((tb, vw), F32)],
        compiler_params=_params(("parallel", "arbitrary")),
        name="gdn_delta_rule",
    )(qkv, qkv, qkv, proj, ba_g, alog_g, dt_g, norm_w.reshape(1, d))


def _split3_bf16(x):
    hi = x.astype(BF16).astype(F32)
    r1 = x - hi
    mid = r1.astype(BF16).astype(F32)
    return hi, mid, r1 - mid


def _fox_pre_kernel(q_ref, k_ref, v_ref, c_ref, qw_ref, kw_ref, qa_ref, ka_ref, vt_ref, *, n_heads):
    d = HEAD_DIM
    tt = q_ref.shape[0]
    c2 = c_ref[...] * LOG2E
    lane = lax.broadcasted_iota(I32, (tt, d), 1)
    qw = qw_ref[...] * (d ** -0.5 * LOG2E)
    kw = kw_ref[...]
    for h in range(n_heads):
        qh = q_ref[:, h * d:(h + 1) * d]
        kh = k_ref[:, h * d:(h + 1) * d]
        qn = qh * lax.rsqrt(jnp.mean(qh * qh, axis=1, keepdims=True) + NORM_EPS) * qw
        kn = kh * lax.rsqrt(jnp.mean(kh * kh, axis=1, keepdims=True) + NORM_EPS) * kw
        hi, mid, lo = _split3_bf16(c2[:, h:h + 1])
        aq = jnp.where(lane == 0, hi, jnp.where(lane == 1, mid, jnp.where(lane == 2, lo,
                                                                          jnp.where(lane < 6, 1.0, 0.0))))
        ak = jnp.where(lane < 3, 1.0, jnp.where(lane == 3, -hi, jnp.where(lane == 4, -mid,
                                                                           jnp.where(lane == 5, -lo, 0.0))))
        qa_ref[:, 2 * h * d:(2 * h + 1) * d] = qn.astype(BF16)
        qa_ref[:, (2 * h + 1) * d:(2 * h + 2) * d] = aq.astype(BF16)
        ka_ref[:, 2 * h * d:(2 * h + 1) * d] = kn.astype(BF16)
        ka_ref[:, (2 * h + 1) * d:(2 * h + 2) * d] = ak.astype(BF16)
    vt_ref[...] = v_ref[...].T.astype(BF16)


def fox_preprocess(proj, c_pad, q_norm_w, k_norm_w, n_heads, *, tt_pref=256):
    s = proj.shape[0]
    d = HEAD_DIM
    dim = n_heads * d
    tt = _tile(s, tt_pref)
    return pl.pallas_call(
        functools.partial(_fox_pre_kernel, n_heads=n_heads),
        grid=(s // tt,),
        in_specs=[pl.BlockSpec((tt, dim), lambda i: (i, 0)),
                  pl.BlockSpec((tt, dim), lambda i: (i, 1)),
                  pl.BlockSpec((tt, dim), lambda i: (i, 2)),
                  pl.BlockSpec((tt, LANES), lambda i: (i, 0)),
                  pl.BlockSpec((1, d), lambda i: (0, 0)),
                  pl.BlockSpec((1, d), lambda i: (0, 0))],
        out_specs=[pl.BlockSpec((tt, 2 * dim), lambda i: (i, 0)),
                   pl.BlockSpec((tt, 2 * dim), lambda i: (i, 0)),
                   pl.BlockSpec((dim, tt), lambda i: (0, i))],
        out_shape=[jax.ShapeDtypeStruct((s, 2 * dim), BF16), jax.ShapeDtypeStruct((s, 2 * dim), BF16),
                   jax.ShapeDtypeStruct((dim, s), BF16)],
        compiler_params=_params(("parallel",)),
        name="fox_qkv_prep",
    )(proj, proj, proj, c_pad, q_norm_w.reshape(1, d), k_norm_w.reshape(1, d))


def _fox_cumgate_kernel(f_ref, b_ref, o_ref, carry_ref):
    i = pl.program_id(0)
    tt = f_ref.shape[0]

    @pl.when(i == 0)
    def _():
        carry_ref[...] = jnp.zeros_like(carry_ref)

    x = f_ref[...] + b_ref[...]
    log_f = jnp.minimum(x, 0.0) - jnp.log(1.0 + jnp.exp(-jnp.abs(x)))
    csum = _chunk_cumsum(log_f, tt) + carry_ref[...]
    o_ref[...] = csum
    carry_ref[...] = csum[tt - 1:tt, :]


def fox_cumulative_gate(f_logits, b_row, *, tt_pref=512):
    s = f_logits.shape[0]
    tt = _tile(s, tt_pref)
    return pl.pallas_call(
        _fox_cumgate_kernel,
        grid=(s // tt,),
        in_specs=[pl.BlockSpec((tt, LANES), lambda i: (i, 0)),
                  pl.BlockSpec((1, LANES), lambda i: (0, 0))],
        out_specs=pl.BlockSpec((tt, LANES), lambda i: (i, 0)),
        out_shape=jax.ShapeDtypeStruct((s, LANES), F32),
        scratch_shapes=[pltpu.VMEM((1, LANES), F32)],
        compiler_params=_params(("arbitrary",)),
        name="fox_cumulative_gate",
    )(f_logits, b_row)


def _fox_attn_kernel(q_ref, k_ref, vt_ref, gate_ref, o_ref, m_ref, l_ref, acc_ref):
    qi = pl.program_id(1)
    tq = q_ref.shape[0]
    tk = tq
    q = q_ref[...]

    m_ref[...] = jnp.full_like(m_ref, NEG_BIG)
    l_ref[...] = jnp.zeros_like(l_ref)
    acc_ref[...] = jnp.zeros_like(acc_ref)

    def block(kj, masked, state):
        m_old, l_old, acc_old = state
        k0 = pl.multiple_of(kj * tk, tk)
        kb = k_ref[pl.ds(k0, tk), :]
        st = lax.dot_general(kb, q, (((1,), (1,)), ((), ())), preferred_element_type=F32)
        if masked:
            kpos = lax.broadcasted_iota(I32, (tk, tq), 0)
            qpos = lax.broadcasted_iota(I32, (tk, tq), 1)
            st = jnp.where(kpos <= qpos, st, -jnp.inf)
        m_new = jnp.maximum(m_old, jnp.max(st, axis=0, keepdims=True))
        alpha = jnp.exp2(m_old - m_new)
        p = jnp.exp2(st - m_new)
        l_new = alpha * l_old + jnp.sum(p, axis=0, keepdims=True)
        vb = vt_ref[:, pl.ds(k0, tk)]
        acc_new = alpha * acc_old + jnp.dot(vb, p.astype(BF16), preferred_element_type=F32)
        return m_new, l_new, acc_new

    def load_state():
        return m_ref[...], l_ref[...], acc_ref[...]

    def store_state(state):
        m_ref[...], l_ref[...], acc_ref[...] = state

    def full_body(kj, carry):
        store_state(block(kj, False, load_state()))
        return carry

    lax.fori_loop(0, qi, full_body, 0)
    _, l_fin, acc_fin = block(qi, True, load_state())

    o_t = acc_fin / l_fin
    o_ref[...] = (o_t.T * _sigmoid(gate_ref[...])).astype(BF16)


def fox_attention(qa, ka, v_t, proj, n_heads, *, tq_pref=1024):
    s = qa.shape[0]
    d = HEAD_DIM
    tq = _tile(s, tq_pref)
    return pl.pallas_call(
        _fox_attn_kernel,
        grid=(n_heads, s // tq),
        in_specs=[pl.BlockSpec((tq, 2 * d), lambda h, i: (i, h)),
                  pl.BlockSpec((s, 2 * d), lambda h, i: (0, h)),
                  pl.BlockSpec((d, s), lambda h, i: (h, 0)),
                  pl.BlockSpec((tq, d), lambda h, i: (i, 3 * n_heads + h))],
        out_specs=pl.BlockSpec((tq, d), lambda h, i: (i, h)),
        out_shape=jax.ShapeDtypeStruct((s, n_heads * d), BF16),
        scratch_shapes=[pltpu.VMEM((1, tq), F32), pltpu.VMEM((1, tq), F32), pltpu.VMEM((d, tq), F32)],
        compiler_params=_params(("parallel", "arbitrary")),
        name="fox_attention",
    )(qa, ka, v_t, proj)


def _split_bf16(x):
    hi = x.astype(BF16)
    lo = (x - hi.astype(F32)).astype(BF16)
    return hi, lo


def _first_argmax(vals, mask, lane_f):
    v = jnp.where(mask, vals, NEG_BIG)
    vmax = jnp.max(v, axis=1, keepdims=True)
    hit = jnp.logical_and(mask, v == vmax)
    idx = jnp.min(jnp.where(hit, lane_f, float(LANES)), axis=1, keepdims=True)
    return vmax, idx


def _router_kernel(x_ref, w_ref, b_ref, ids_ref, wts_ref, rank_ref, cnt_ref, carry_ref, *, n_groups, n_exp):
    i = pl.program_id(0)
    tm = x_ref.shape[0]

    @pl.when(i == 0)
    def _():
        carry_ref[...] = jnp.zeros_like(carry_ref)

    xh, xl = _split_bf16(x_ref[...])
    wh, wl = _split_bf16(w_ref[...])
    lg = (jnp.dot(xh, wh, preferred_element_type=F32) + jnp.dot(xh, wl, preferred_element_type=F32)
          + jnp.dot(xl, wh, preferred_element_type=F32)) + b_ref[...]
    lane = lax.broadcasted_iota(I32, lg.shape, 1)
    lane_f = lane.astype(F32)

    gmask = lane < n_groups
    gmax, gidx = _first_argmax(lg, gmask, lane_f)
    gsum = jnp.sum(jnp.where(gmask, jnp.exp(jnp.where(gmask, lg, NEG_BIG) - gmax), 0.0), axis=1, keepdims=True)
    p_group = 1.0 / gsum

    e_lo = float(n_groups) + gidx * float(n_exp)
    emask = jnp.logical_and(lane_f >= e_lo, lane_f < e_lo + float(n_exp))
    t1, i1 = _first_argmax(lg, emask, lane_f)
    emask2 = jnp.logical_and(emask, lane_f != i1)
    t2, i2 = _first_argmax(lg, emask2, lane_f)
    e21 = jnp.exp(t2 - t1)
    p1 = p_group / (1.0 + e21)
    p2 = p_group * e21 / (1.0 + e21)
    id1 = i1 - float(n_groups)
    id2 = i2 - float(n_groups)

    oh = jnp.where(lane_f == id1, 1.0, 0.0) + jnp.where(lane_f == id2, 1.0, 0.0)
    r = lax.broadcasted_iota(I32, (tm, tm), 0)
    cc = lax.broadcasted_iota(I32, (tm, tm), 1)
    tril = jnp.where(r > cc, 1.0, 0.0).astype(BF16)
    before = jnp.dot(tril, oh.astype(BF16), preferred_element_type=F32) + carry_ref[...]
    rank1 = jnp.sum(jnp.where(lane_f == id1, before, 0.0), axis=1, keepdims=True)
    rank2 = jnp.sum(jnp.where(lane_f == id2, before, 0.0), axis=1, keepdims=True)
    new_carry = carry_ref[...] + jnp.sum(oh, axis=0, keepdims=True)
    carry_ref[...] = new_carry
    cnt_ref[...] = jnp.broadcast_to(new_carry, cnt_ref.shape).astype(I32)

    first = lane == 0
    second = lane == 1
    ids_ref[...] = jnp.where(first, id1, jnp.where(second, id2, 0.0)).astype(I32)
    wts_ref[...] = jnp.where(first, p1, jnp.where(second, p2, 0.0))
    rank_ref[...] = jnp.where(first, rank1, jnp.where(second, rank2, 0.0)).astype(I32)


def moe_router(x, w_pad, b_pad, n_groups, n_exp, *, tm_pref=512):
    s, dm = x.shape
    tm = _tile(s, tm_pref)
    return pl.pallas_call(
        functools.partial(_router_kernel, n_groups=n_groups, n_exp=n_exp),
        grid=(s // tm,),
        in_specs=[pl.BlockSpec((tm, dm), lambda i: (i, 0)),
                  pl.BlockSpec((dm, LANES), lambda i: (0, 0)),
                  pl.BlockSpec((1, LANES), lambda i: (0, 0))],
        out_specs=[pl.BlockSpec((tm, LANES), lambda i: (i, 0)),
                   pl.BlockSpec((tm, LANES), lambda i: (i, 0)),
                   pl.BlockSpec((tm, LANES), lambda i: (i, 0)),
                   pl.BlockSpec((SUBLANES, LANES), lambda i: (0, 0))],
        out_shape=[jax.ShapeDtypeStruct((s, LANES), I32), jax.ShapeDtypeStruct((s, LANES), F32),
                   jax.ShapeDtypeStruct((s, LANES), I32), jax.ShapeDtypeStruct((SUBLANES, LANES), I32)],
        scratch_shapes=[pltpu.VMEM((1, LANES), F32)],
        compiler_params=_params(("arbitrary",)),
        name="moe_router",
    )(x, w_pad, b_pad)


ROW_DMA_PRIORITIES = (0, 1)


def _start_row_gather(src_hbm, dst, sem, idx_ref, base):
    for r in range(dst.shape[0]):
        tok = idx_ref[base + r]
        pltpu.make_async_copy(src_hbm.at[pl.ds(tok, 1), :], dst.at[pl.ds(r, 1), :], sem).start(
            priority=ROW_DMA_PRIORITIES[r % 2])


def _wait_row_gather(dst, sem):
    pltpu.make_async_copy(dst, dst, sem).wait()


def _moe_ffn_kernel(te_ref, nu_ref, src_ref, x_hbm, wg_ref, wu_ref, wd_ref, o_ref,
                    xbuf, sem, wgb, wub, wdb):
    i = pl.program_id(0)
    tm = o_ref.shape[0]
    n_used = nu_ref[0]

    @pl.when(i == 0)
    def _():
        _start_row_gather(x_hbm, xbuf.at[0], sem.at[0], src_ref, 0)

    def used_tile(slot):
        is_new = jnp.logical_or(i == 0, te_ref[i] != te_ref[jnp.maximum(i - 1, 0)])

        @pl.when(is_new)
        def _():
            wgb[...] = wg_ref[...].astype(BF16)
            wub[...] = wu_ref[...].astype(BF16)
            wdb[...] = wd_ref[...].astype(BF16)

        _wait_row_gather(xbuf.at[slot], sem.at[slot])
        _start_row_gather(x_hbm, xbuf.at[1 - slot], sem.at[1 - slot], src_ref, (i + 1) * tm)
        x = xbuf[slot].astype(BF16)
        gate = jnp.dot(x, wgb[...], preferred_element_type=F32)
        up = jnp.dot(x, wub[...], preferred_element_type=F32)
        hmid = (gate * _sigmoid(gate) * up).astype(BF16)
        o_ref[...] = jnp.dot(hmid, wdb[...], preferred_element_type=F32)

    for slot in range(2):
        pl.when(jnp.logical_and(i < n_used, i % 2 == slot))(functools.partial(used_tile, slot))

    @pl.when(i >= n_used)
    def _():
        for slot in range(2):
            @pl.when(jnp.logical_and(i == n_used, i % 2 == slot))
            def _():
                _wait_row_gather(xbuf.at[slot], sem.at[slot])

        o_ref[...] = jnp.zeros_like(o_ref)


def moe_expert_ffn(x, w_gate, w_up, w_down, layer, tile_expert, n_used, src_tok, tm):
    s, dm = x.shape
    n_exp_total = MOE_GROUPS * MOE_EXPERTS_PER_GROUP
    fe = w_gate.shape[-1]
    n_tiles = tile_expert.shape[0]
    wg = w_gate.reshape(-1, dm, fe)
    wu = w_up.reshape(-1, dm, fe)
    wd = w_down.reshape(-1, fe, dm)
    base = layer * n_exp_total
    grid_spec = pltpu.PrefetchScalarGridSpec(
        num_scalar_prefetch=3,
        grid=(n_tiles,),
        in_specs=[pl.BlockSpec(memory_space=pl.ANY),
                  pl.BlockSpec((None, dm, fe), lambda i, te, nu, src: (base + te[i], 0, 0)),
                  pl.BlockSpec((None, dm, fe), lambda i, te, nu, src: (base + te[i], 0, 0)),
                  pl.BlockSpec((None, fe, dm), lambda i, te, nu, src: (base + te[i], 0, 0))],
        out_specs=pl.BlockSpec((tm, dm), lambda i, te, nu, src: (i, 0)),
        scratch_shapes=[pltpu.VMEM((2, tm, dm), F32),
                        pltpu.SemaphoreType.DMA((2,)),
                        pltpu.VMEM((dm, fe), BF16),
                        pltpu.VMEM((dm, fe), BF16),
                        pltpu.VMEM((fe, dm), BF16)])
    return pl.pallas_call(
        _moe_ffn_kernel,
        grid_spec=grid_spec,
        out_shape=jax.ShapeDtypeStruct((n_tiles * tm, dm), F32),
        compiler_params=_params(("arbitrary",)),
        name="moe_expert_ffn",
    )(tile_expert, n_used, src_tok, x, wg, wu, wd)


def _moe_combine_kernel(pos_ref, y_hbm, x_ref, wts_ref, g_ref, b_ref, o_ref, ob_ref, ybuf, sem, *, alpha):
    i = pl.program_id(0)
    n = pl.num_programs(0)
    tm = x_ref.shape[0]

    def gather(tile, sl):
        for kk in range(MOE_TOP_K):
            _start_row_gather(y_hbm, ybuf.at[sl, kk], sem.at[sl], pos_ref, (kk * n + tile) * tm)

    @pl.when(i == 0)
    def _():
        gather(0, 0)

    def tile_body(slot, prefetch):
        _wait_row_gather(ybuf.at[slot], sem.at[slot])
        if prefetch:
            gather(i + 1, 1 - slot)
        wts = wts_ref[...]
        mix = wts[:, 0:1] * ybuf[slot, 0] + wts[:, 1:2] * ybuf[slot, 1]
        y = alpha * x_ref[...] + mix
        out = _layernorm_rows(y, g_ref[...], b_ref[...])
        o_ref[...] = out
        ob_ref[...] = out.astype(BF16)

    for slot in range(2):
        pl.when(jnp.logical_and(i + 1 < n, i % 2 == slot))(functools.partial(tile_body, slot, True))
        pl.when(jnp.logical_and(i + 1 == n, i % 2 == slot))(functools.partial(tile_body, slot, False))


def moe_combine_layernorm(y_sorted, pos_km, x, wts, g, b, alpha, *, tm_pref=256):
    s, dm = x.shape
    tm = _tile(s, tm_pref)
    grid_spec = pltpu.PrefetchScalarGridSpec(
        num_scalar_prefetch=1,
        grid=(s // tm,),
        in_specs=[pl.BlockSpec(memory_space=pl.ANY),
                  pl.BlockSpec((tm, dm), lambda i, pos: (i, 0)),
                  pl.BlockSpec((tm, LANES), lambda i, pos: (i, 0)),
                  pl.BlockSpec((1, dm), lambda i, pos: (0, 0)),
                  pl.BlockSpec((1, dm), lambda i, pos: (0, 0))],
        out_specs=[pl.BlockSpec((tm, dm), lambda i, pos: (i, 0)),
                   pl.BlockSpec((tm, dm), lambda i, pos: (i, 0))],
        scratch_shapes=[pltpu.VMEM((2, MOE_TOP_K, tm, dm), F32),
                        pltpu.SemaphoreType.DMA((2,))])
    return pl.pallas_call(
        functools.partial(_moe_combine_kernel, alpha=alpha),
        grid_spec=grid_spec,
        out_shape=[jax.ShapeDtypeStruct((s, dm), F32), jax.ShapeDtypeStruct((s, dm), BF16)],
        compiler_params=_params(("arbitrary",)),
        name="moe_combine_ln",
    )(pos_km, y_sorted, x, wts, g.reshape(1, dm), b.reshape(1, dm))


def _pad_cols(w, width=LANES):
    return jnp.pad(w, ((0, 0), (0, width - w.shape[1])))


def _moe_layer(x, x_bf16, layer, w_rg, b_rg, w_re, b_re, w_gate, w_up, w_down, ln_g, ln_b, alpha, *, tm=256):
    del x_bf16
    s, dm = x.shape
    n_groups, _, n_exp = w_re.shape
    n_total = n_groups * n_exp
    w_r = _pad_cols(jnp.concatenate([w_rg, jnp.transpose(w_re, (1, 0, 2)).reshape(dm, n_total)], axis=1))
    b_r = _pad_cols(jnp.concatenate([b_rg, b_re.reshape(n_total)])[None, :])
    ids, wts, ranks, counts = moe_router(x, w_r, b_r, n_groups, n_exp)

    n_tiles = (s * MOE_TOP_K) // tm + n_total + 1
    counts = counts[0, :n_total]
    padded = ((counts + tm - 1) // tm) * tm
    ends = jnp.cumsum(padded)
    starts = ends - padded
    ids2 = ids[:, :MOE_TOP_K]
    pos = starts[ids2] + ranks[:, :MOE_TOP_K]
    tok = jnp.broadcast_to(jnp.arange(s, dtype=I32)[:, None], (s, MOE_TOP_K))
    src_tok = jnp.zeros((n_tiles * tm,), I32).at[pos.reshape(-1)].set(tok.reshape(-1))
    n_used = (ends[-1] // tm).astype(I32)
    tile_start = jnp.arange(n_tiles, dtype=I32) * tm
    tile_expert = jnp.minimum(jnp.searchsorted(ends, tile_start, side="right"), n_total - 1).astype(I32)
    last_expert = tile_expert[jnp.maximum(n_used - 1, 0)]
    tile_expert = jnp.where(jnp.arange(n_tiles) < n_used, tile_expert, last_expert)

    y_sorted = moe_expert_ffn(x, w_gate, w_up, w_down, layer, tile_expert, n_used.reshape(1), src_tok, tm)
    pos_km = jnp.transpose(pos.reshape(s // tm, tm, MOE_TOP_K), (2, 0, 1)).reshape(-1).astype(I32)
    return moe_combine_layernorm(y_sorted, pos_km, x, wts, ln_g, ln_b, alpha, tm_pref=tm)


def kernel(x, dn_w_in, dn_conv_w, dn_A_log, dn_dt_bias, dn_norm_w, dn_w_out, fox_w_in, fox_b_f, fox_q_norm_w,
           fox_k_norm_w, fox_w_out, ln1_g, ln1_b, ln2_g, ln2_b, moe_w_rg, moe_b_rg, moe_w_re, moe_b_re,
           moe_w_gate, moe_w_up, moe_w_down):
    batch, s, dm = x.shape
    assert batch == 1
    depth = ln1_g.shape[0]
    alpha = (2 * depth) ** 0.25
    x2d = x.reshape(s, dm)
    xb = x2d.astype(BF16)
    d = HEAD_DIM
    hb = 8

    for i in range(depth):
        j = i // 2
        if i % 2 == 0:
            n_v_heads = dn_A_log.shape[1]
            qk_dim = (n_v_heads // 2) * d
            v_dim = n_v_heads * d
            conv_dim = 2 * qk_dim + v_dim
            w_in_t = jnp.transpose(dn_w_in[j])
            proj = matmul(xb, w_in_t, conv_dim + v_dim)
            hg = n_v_heads // hb
            w_b = w_in_t[conv_dim + v_dim:conv_dim + v_dim + n_v_heads].reshape(hg, hb, dm)
            w_a = w_in_t[conv_dim + v_dim + n_v_heads:].reshape(hg, hb, dm)
            w_ba = jnp.pad(jnp.concatenate([w_b, w_a], axis=1), ((0, 0), (0, LANES - 2 * hb), (0, 0)))
            ba_g = matmul(xb, w_ba.reshape(hg * LANES, dm), hg * LANES)
            zeros = jnp.zeros((hg, hb), F32)
            alog_g = jnp.pad(jnp.concatenate([zeros, dn_A_log[j].reshape(hg, hb)], axis=1),
                             ((0, 0), (0, LANES - 2 * hb))).reshape(hg, 1, LANES)
            dt_g = jnp.pad(jnp.concatenate([zeros, dn_dt_bias[j].reshape(hg, hb)], axis=1),
                           ((0, 0), (0, LANES - 2 * hb))).reshape(hg, 1, LANES)
            qkv = gdn_preprocess(proj, dn_conv_w[j], qk_dim)
            mixed = gdn_delta_rule(qkv, proj, ba_g, alog_g, dt_g, dn_norm_w[j], n_v_heads, hb=hb)
            w_out = dn_w_out[j]
        else:
            n_heads = fox_b_f.shape[1]
            dim = n_heads * d
            w_in_t = jnp.transpose(fox_w_in[j])
            proj = matmul(xb, w_in_t, 4 * dim)
            w_f = jnp.pad(w_in_t[4 * dim:], ((0, LANES - n_heads), (0, 0)))
            f_logits = matmul(xb, w_f, LANES, tn_pref=LANES)
            c_pad = fox_cumulative_gate(f_logits, _pad_cols(fox_b_f[j][None, :]))
            qa, ka, v_t = fox_preprocess(proj, c_pad, fox_q_norm_w[j], fox_k_norm_w[j], n_heads)
            mixed = fox_attention(qa, ka, v_t, proj, n_heads)
            w_out = fox_w_out[j]
        x2d, xb = matmul_residual_layernorm(mixed, w_out.astype(BF16), x2d, ln1_g[i], ln1_b[i], alpha)
        x2d, xb = _moe_layer(x2d, xb, i, moe_w_rg[i], moe_b_rg[i], moe_w_re[i], moe_b_re[i],
                             moe_w_gate, moe_w_up, moe_w_down, ln2_g[i], ln2_b[i], alpha)
    return x2d.reshape(batch, s, dm)
```

```python
import functools
import math

import jax
import jax.numpy as jnp
from jax import lax
from jax.experimental import pallas as pl
from jax.experimental.pallas import tpu as pltpu

F32 = jnp.float32
BF16 = jnp.bfloat16
I32 = jnp.int32

LANES = 128
SUBLANES = 8
V7X_VMEM_BYTES = 64 * 1024 * 1024
VMEM_LIMIT = V7X_VMEM_BYTES - 8 * 1024 * 1024

HEAD_DIM = 128
SOLVE_CHUNK = 128
DN_CONV_WIDTH = 4
MOE_GROUPS = 4
MOE_EXPERTS_PER_GROUP = 8
MOE_TOP_K = 2
NORM_EPS = 1e-6
NEG_BIG = -1e30
LOG2E = 1.4426950408889634


def _params(sem, vmem=VMEM_LIMIT):
    return pltpu.CompilerParams(dimension_semantics=sem, vmem_limit_bytes=vmem)


def _tile(dim, pref):
    t = min(dim, pref)
    assert dim % t == 0, (dim, pref)
    return t


def _softplus(x):
    return jnp.maximum(x, 0.0) + jnp.log(1.0 + jnp.exp(-jnp.abs(x)))


def _sigmoid(x):
    return 1.0 / (1.0 + jnp.exp(-x))


def _mm_kernel(a_ref, wt_ref, o_ref):
    o_ref[...] = lax.dot_general(a_ref[...], wt_ref[...].astype(BF16), (((1,), (1,)), ((), ())),
                                 preferred_element_type=F32)


def _mm_cast_kernel(a_ref, wt_ref, o_ref, ab_ref):
    @pl.when(pl.program_id(1) == 0)
    def _():
        ab_ref[...] = a_ref[...].astype(BF16)

    o_ref[...] = lax.dot_general(ab_ref[...], wt_ref[...].astype(BF16), (((1,), (1,)), ((), ())),
                                 preferred_element_type=F32)


def matmul(a, w_t, n_out, *, tm_pref=2048, tn_pref=512):
    m, k = a.shape
    cast_a = a.dtype != BF16
    tm = _tile(m, tm_pref // 2 if cast_a else tm_pref)
    tn = _tile(n_out, tn_pref)
    return pl.pallas_call(
        _mm_cast_kernel if cast_a else _mm_kernel,
        grid=(m // tm, n_out // tn),
        in_specs=[pl.BlockSpec((tm, k), lambda i, j: (i, 0)),
                  pl.BlockSpec((tn, k), lambda i, j: (j, 0))],
        out_specs=pl.BlockSpec((tm, tn), lambda i, j: (i, j)),
        out_shape=jax.ShapeDtypeStruct((m, n_out), F32),
        scratch_shapes=[pltpu.VMEM((tm, k), BF16)] if cast_a else [],
        compiler_params=_params(("parallel", "arbitrary" if cast_a else "parallel")),
        name="proj_matmul",
    )(a, w_t)


def _layernorm_rows(y, g, b):
    mu = jnp.mean(y, axis=1, keepdims=True)
    yc = y - mu
    var = jnp.mean(yc * yc, axis=1, keepdims=True)
    return yc * lax.rsqrt(var + NORM_EPS) * g + b


def _mm_res_ln_kernel(a_ref, w_ref, res_ref, g_ref, b_ref, o_ref, ob_ref, *, alpha):
    y = alpha * res_ref[...] + jnp.dot(a_ref[...], w_ref[...], preferred_element_type=F32)
    out = _layernorm_rows(y, g_ref[...], b_ref[...])
    o_ref[...] = out
    ob_ref[...] = out.astype(BF16)


def matmul_residual_layernorm(a, w_bf16, res, g, b, alpha, *, tm_pref=512):
    m, k = a.shape
    n = w_bf16.shape[1]
    tm = _tile(m, tm_pref)
    return pl.pallas_call(
        functools.partial(_mm_res_ln_kernel, alpha=alpha),
        grid=(m // tm,),
        in_specs=[pl.BlockSpec((tm, k), lambda i: (i, 0)),
                  pl.BlockSpec((k, n), lambda i: (0, 0), pipeline_mode=pl.Buffered(1)),
                  pl.BlockSpec((tm, n), lambda i: (i, 0)),
                  pl.BlockSpec((1, n), lambda i: (0, 0)),
                  pl.BlockSpec((1, n), lambda i: (0, 0))],
        out_specs=[pl.BlockSpec((tm, n), lambda i: (i, 0)),
                   pl.BlockSpec((tm, n), lambda i: (i, 0))],
        out_shape=[jax.ShapeDtypeStruct((m, n), F32), jax.ShapeDtypeStruct((m, n), BF16)],
        compiler_params=_params(("parallel",)),
        name="out_proj_res_ln",
    )(a, w_bf16, res, g.reshape(1, n), b.reshape(1, n))


def _gdn_pre_kernel(cur_ref, halo_ref, w_ref, o_ref, ext_ref, *, n_qk_blocks, q_scale):
    i = pl.program_id(0)
    j = pl.program_id(1)
    tt, cb = cur_ref.shape
    keep = (i > 0).astype(F32)
    ext_ref[0:SUBLANES, :] = halo_ref[...] * keep
    ext_ref[SUBLANES:SUBLANES + tt, :] = cur_ref[...]
    w = w_ref[...]
    acc = cur_ref[...] * w[DN_CONV_WIDTH - 1:DN_CONV_WIDTH]
    for kk in range(DN_CONV_WIDTH - 1):
        off = SUBLANES - (DN_CONV_WIDTH - 1) + kk
        acc = acc + ext_ref[off:off + tt, :] * w[kk:kk + 1]
    y = acc * _sigmoid(acc)

    def write_normed(scale):
        for h in range(cb // HEAD_DIM):
            yh = y[:, h * HEAD_DIM:(h + 1) * HEAD_DIM]
            r = lax.rsqrt(jnp.sum(yh * yh, axis=1, keepdims=True) + NORM_EPS)
            o_ref[:, h * HEAD_DIM:(h + 1) * HEAD_DIM] = yh * (r * scale if scale != 1.0 else r)

    @pl.when(j < n_qk_blocks)
    def _():
        write_normed(q_scale)

    @pl.when(jnp.logical_and(j >= n_qk_blocks, j < 2 * n_qk_blocks))
    def _():
        write_normed(1.0)

    @pl.when(j >= 2 * n_qk_blocks)
    def _():
        o_ref[...] = y


def gdn_preprocess(proj, conv_w, qk_dim, *, tt_pref=512, cb_pref=512):
    s = proj.shape[0]
    conv_dim = conv_w.shape[1]
    tt = _tile(s, tt_pref)
    cb = _tile(qk_dim, cb_pref)
    assert conv_dim % cb == 0 and tt % SUBLANES == 0
    hb = tt // SUBLANES
    return pl.pallas_call(
        functools.partial(_gdn_pre_kernel, n_qk_blocks=qk_dim // cb, q_scale=HEAD_DIM ** -0.5),
        grid=(s // tt, conv_dim // cb),
        in_specs=[pl.BlockSpec((tt, cb), lambda i, j: (i, j)),
                  pl.BlockSpec((SUBLANES, cb), lambda i, j: (jnp.maximum(i * hb - 1, 0), j)),
                  pl.BlockSpec((DN_CONV_WIDTH, cb), lambda i, j: (0, j))],
        out_specs=pl.BlockSpec((tt, cb), lambda i, j: (i, j)),
        out_shape=jax.ShapeDtypeStruct((s, conv_dim), F32),
        scratch_shapes=[pltpu.VMEM((tt + SUBLANES, cb), F32)],
        compiler_params=_params(("parallel", "parallel")),
        name="gdn_conv_norm",
    )(proj, proj, conv_w)


def _bmm(a, b):
    return jnp.einsum("nij,njk->nik", a.astype(BF16), b.astype(BF16), preferred_element_type=F32)


def _bmm_nt(a, b):
    return jnp.einsum("nid,njd->nij", a.astype(BF16), b.astype(BF16), preferred_element_type=F32)


def _chunk_cumsum(x, chunk, reverse=False):
    rows = x.shape[0]
    pos = lax.broadcasted_iota(I32, x.shape, 0) % chunk
    step = 1
    while step < chunk:
        if reverse:
            shifted = pltpu.roll(x, rows - step, axis=0)
            x = x + jnp.where(pos + step < chunk, shifted, 0.0)
        else:
            shifted = pltpu.roll(x, step, axis=0)
            x = x + jnp.where(pos >= step, shifted, 0.0)
        step *= 2
    return x


def _cat_lanes(a, b):
    return jnp.concatenate([a, b], axis=-1)


def _block_diag(a, b):
    return jnp.concatenate([_cat_lanes(a, jnp.zeros_like(b)), _cat_lanes(jnp.zeros_like(a), b)], axis=-2)


def _unit_lower_inverse_pair(lower_a, lower_b, eye):
    c = lower_a.shape[-1]
    pa, pb = lower_a.astype(BF16), lower_b.astype(BF16)
    tinv = _cat_lanes(eye - lower_a, eye - lower_b)
    span = 2
    while span < c:
        power = jnp.einsum("nij,njk->nik", _cat_lanes(pa, pb), _block_diag(pa, pb), preferred_element_type=F32)
        pa, pb = power[..., :c].astype(BF16), power[..., c:].astype(BF16)
        tinv = tinv + jnp.einsum("nij,njk->nik", tinv.astype(BF16), _block_diag(pa, pb),
                                 preferred_element_type=F32)
        span *= 2
    return tinv[..., :c], tinv[..., c:]


def _delta_kernel(q_ref, k_ref, v_ref, z_ref, ba_ref, alog_ref, dt_ref, nw_ref, o_ref,
                  state_ref, u_ref, wq_ref, ak_ref, egl_ref, osc_ref, *, hb):
    t = pl.program_id(1)
    tb = q_ref.shape[0]
    c = SOLVE_CHUNK
    nc = tb // c
    d = HEAD_DIM

    @pl.when(t == 0)
    def _():
        state_ref[...] = jnp.zeros_like(state_ref)

    ba = ba_ref[...]
    beta_all = _sigmoid(ba)
    g_all = -jnp.exp(alog_ref[...]) * _softplus(ba + dt_ref[...])
    big_g = _chunk_cumsum(g_all, c)
    tail_g = _chunk_cumsum(g_all, c, reverse=True) - g_all
    e_g = jnp.exp(big_g)
    e_tail = jnp.exp(tail_g)
    e_tot = jnp.exp(big_g + tail_g)
    big_g_t = big_g.T

    row = lax.broadcasted_iota(I32, (c, c), 0)
    col = lax.broadcasted_iota(I32, (c, c), 1)
    causal = row >= col
    strict = row > col
    eye = (row == col).astype(F32)[None]

    def head_terms(hh):
        gi = hb + hh
        gcol_b = jnp.broadcast_to(big_g[:, gi:gi + 1], (tb, c))
        decays = []
        for n in range(nc):
            diff = gcol_b[n * c:(n + 1) * c, :] - big_g_t[gi:gi + 1, n * c:(n + 1) * c]
            decays.append(jnp.where(causal, jnp.exp(jnp.where(causal, diff, 0.0)), 0.0))
        return beta_all[:, hh:hh + 1], e_g[:, gi:gi + 1], e_tail[:, gi:gi + 1], jnp.stack(decays, axis=0)

    for p in range(hb // 2):
        heads = (2 * p, 2 * p + 1)
        qh = q_ref[:, p * d:(p + 1) * d]
        kh = k_ref[:, p * d:(p + 1) * d]
        terms = [head_terms(hh) for hh in heads]
        kbs = [kh * tm_[0] for tm_ in terms]
        stacked = jnp.concatenate([kbs[0].reshape(nc, c, d), kbs[1].reshape(nc, c, d), qh.reshape(nc, c, d)],
                                  axis=1)
        prod = _bmm_nt(stacked, kh.reshape(nc, c, d))
        qk = prod[:, 2 * c:]
        lowers = [jnp.where(strict[None], prod[:, i * c:(i + 1) * c] * terms[i][3], 0.0) for i in range(2)]
        tinvs = _unit_lower_inverse_pair(lowers[0], lowers[1], eye)
        packed = []
        for i, hh in enumerate(heads):
            bcol, egcol, etcol, decay = terms[i]
            vh = v_ref[:, hh * d:(hh + 1) * d]
            rhs = _cat_lanes((vh * bcol).reshape(nc, c, d), (kbs[i] * egcol).reshape(nc, c, d))
            sol = _bmm(tinvs[i], rhs)
            u_ref[hh] = sol[..., :d]
            wq_i = jnp.concatenate([sol[..., d:], (qh * egcol).reshape(nc, c, d)], axis=1).astype(BF16)
            kd_t = (kh * etcol).T.astype(BF16)
            kdt3 = jnp.stack([kd_t[:, n * c:(n + 1) * c] for n in range(nc)], axis=0)
            packed.append((wq_i, jnp.concatenate([(qk * decay).astype(BF16), kdt3], axis=1)))
            egl_ref[hh] = jnp.broadcast_to(e_tot[:, hb + hh:hb + hh + 1], (tb, d))
        wq_ref[p] = _cat_lanes(packed[0][0], packed[1][0])
        ak_ref[p] = _cat_lanes(packed[0][1], packed[1][1])

    def chunk_step(n, carry):
        r0 = pl.multiple_of(n * c, c)
        for p in range(hb // 2):
            ha, hb_ = 2 * p, 2 * p + 1
            sa, sb = state_ref[ha], state_ref[hb_]
            ws = jnp.dot(wq_ref[p, n], _block_diag(sa.astype(BF16), sb.astype(BF16)),
                         preferred_element_type=F32)
            vna = u_ref[ha, n] - ws[:c, :d]
            vnb = u_ref[hb_, n] - ws[:c, d:]
            r = jnp.dot(ak_ref[p, n], _block_diag(vna.astype(BF16), vnb.astype(BF16)),
                        preferred_element_type=F32)
            osc_ref[pl.ds(r0, c), ha * d:(ha + 1) * d] = ws[c:, :d] + r[:c, :d]
            osc_ref[pl.ds(r0, c), hb_ * d:(hb_ + 1) * d] = ws[c:, d:] + r[:c, d:]
            state_ref[ha] = sa * egl_ref[ha, pl.ds(r0, 1), :] + r[c:, :d]
            state_ref[hb_] = sb * egl_ref[hb_, pl.ds(r0, 1), :] + r[c:, d:]
        return carry

    lax.fori_loop(0, nc, chunk_step, 0)

    nw = nw_ref[...]
    for hh in range(hb):
        o = osc_ref[:, hh * d:(hh + 1) * d]
        z = z_ref[:, hh * d:(hh + 1) * d]
        ms = jnp.mean(o * o, axis=1, keepdims=True)
        y = o * lax.rsqrt(ms + NORM_EPS) * nw * (z * _sigmoid(z))
        o_ref[:, hh * d:(hh + 1) * d] = y.astype(BF16)


def gdn_delta_rule(qkv, proj, ba_g, alog_g, dt_g, norm_w, n_v_heads, *, hb, tb_pref=512):
    s = qkv.shape[0]
    d = HEAD_DIM
    c = SOLVE_CHUNK
    qk_dim = (n_v_heads // 2) * d
    v_dim = n_v_heads * d
    tb = _tile(s, tb_pref)
    nc = tb // c
    hg = n_v_heads // hb
    qw = (hb // 2) * d
    vw = hb * d
    return pl.pallas_call(
        functools.partial(_delta_kernel, hb=hb),
        grid=(hg, s // tb),
        in_specs=[pl.BlockSpec((tb, qw), lambda g, t: (t, g)),
                  pl.BlockSpec((tb, qw), lambda g, t: (t, qk_dim // qw + g)),
                  pl.BlockSpec((tb, vw), lambda g, t: (t, 2 * qk_dim // vw + g)),
                  pl.BlockSpec((tb, vw), lambda g, t: (t, (2 * qk_dim + v_dim) // vw + g)),
                  pl.BlockSpec((tb, LANES), lambda g, t: (t, g)),
                  pl.BlockSpec((None, 1, LANES), lambda g, t: (g, 0, 0)),
                  pl.BlockSpec((None, 1, LANES), lambda g, t: (g, 0, 0)),
                  pl.BlockSpec((1, d), lambda g, t: (0, 0))],
        out_specs=pl.BlockSpec((tb, vw), lambda g, t: (t, g)),
        out_shape=jax.ShapeDtypeStruct((s, v_dim), BF16),
        scratch_shapes=[pltpu.VMEM((hb, d, d), F32),
                        pltpu.VMEM((hb, nc, c, d), F32),
                        pltpu.VMEM((hb // 2, nc, 2 * c, 2 * d), BF16),
                        pltpu.VMEM((hb // 2, nc, c + d, 2 * c), BF16),
                        pltpu.VMEM((hb, tb, d), F32),
                        pltpu.VMEM((tb, vw), F32)],
        compiler_params=_params(("parallel", "arbitrary")),
        name="gdn_delta_rule",
    )(qkv, qkv, qkv, proj, ba_g, alog_g, dt_g, norm_w.reshape(1, d))


def _split3_bf16(x):
    hi = x.astype(BF16).astype(F32)
    r1 = x - hi
    mid = r1.astype(BF16).astype(F32)
    return hi, mid, r1 - mid


def _fox_pre_kernel(q_ref, k_ref, v_ref, c_ref, qw_ref, kw_ref, qa_ref, ka_ref, vt_ref, *, n_heads):
    d = HEAD_DIM
    tt = q_ref.shape[0]
    c2 = c_ref[...] * LOG2E
    lane = lax.broadcasted_iota(I32, (tt, d), 1)
    qw = qw_ref[...] * (d ** -0.5 * LOG2E)
    kw = kw_ref[...]
    for h in range(n_heads):
        qh = q_ref[:, h * d:(h + 1) * d]
        kh = k_ref[:, h * d:(h + 1) * d]
        qn = qh * lax.rsqrt(jnp.mean(qh * qh, axis=1, keepdims=True) + NORM_EPS) * qw
        kn = kh * lax.rsqrt(jnp.mean(kh * kh, axis=1, keepdims=True) + NORM_EPS) * kw
        hi, mid, lo = _split3_bf16(c2[:, h:h + 1])
        aq = jnp.where(lane == 0, hi, jnp.where(lane == 1, mid, jnp.where(lane == 2, lo,
                                                                          jnp.where(lane < 6, 1.0, 0.0))))
        ak = jnp.where(lane < 3, 1.0, jnp.where(lane == 3, -hi, jnp.where(lane == 4, -mid,
                                                                           jnp.where(lane == 5, -lo, 0.0))))
        qa_ref[:, 2 * h * d:(2 * h + 1) * d] = qn.astype(BF16)
        qa_ref[:, (2 * h + 1) * d:(2 * h + 2) * d] = aq.astype(BF16)
        ka_ref[:, 2 * h * d:(2 * h + 1) * d] = kn.astype(BF16)
        ka_ref[:, (2 * h + 1) * d:(2 * h + 2) * d] = ak.astype(BF16)
    vt_ref[...] = v_ref[...].T.astype(BF16)


def fox_preprocess(proj, c_pad, q_norm_w, k_norm_w, n_heads, *, tt_pref=256):
    s = proj.shape[0]
    d = HEAD_DIM
    dim = n_heads * d
    tt = _tile(s, tt_pref)
    return pl.pallas_call(
        functools.partial(_fox_pre_kernel, n_heads=n_heads),
        grid=(s // tt,),
        in_specs=[pl.BlockSpec((tt, dim), lambda i: (i, 0)),
                  pl.BlockSpec((tt, dim), lambda i: (i, 1)),
                  pl.BlockSpec((tt, dim), lambda i: (i, 2)),
                  pl.BlockSpec((tt, LANES), lambda i: (i, 0)),
                  pl.BlockSpec((1, d), lambda i: (0, 0)),
                  pl.BlockSpec((1, d), lambda i: (0, 0))],
        out_specs=[pl.BlockSpec((tt, 2 * dim), lambda i: (i, 0)),
                   pl.BlockSpec((tt, 2 * dim), lambda i: (i, 0)),
                   pl.BlockSpec((dim, tt), lambda i: (0, i))],
        out_shape=[jax.ShapeDtypeStruct((s, 2 * dim), BF16), jax.ShapeDtypeStruct((s, 2 * dim), BF16),
                   jax.ShapeDtypeStruct((dim, s), BF16)],
        compiler_params=_params(("parallel",)),
        name="fox_qkv_prep",
    )(proj, proj, proj, c_pad, q_norm_w.reshape(1, d), k_norm_w.reshape(1, d))


def _fox_cumgate_kernel(f_ref, b_ref, o_ref, carry_ref):
    i = pl.program_id(0)
    tt = f_ref.shape[0]

    @pl.when(i == 0)
    def _():
        carry_ref[...] = jnp.zeros_like(carry_ref)

    x = f_ref[...] + b_ref[...]
    log_f = jnp.minimum(x, 0.0) - jnp.log(1.0 + jnp.exp(-jnp.abs(x)))
    csum = _chunk_cumsum(log_f, tt) + carry_ref[...]
    o_ref[...] = csum
    carry_ref[...] = csum[tt - 1:tt, :]


def fox_cumulative_gate(f_logits, b_row, *, tt_pref=512):
    s = f_logits.shape[0]
    tt = _tile(s, tt_pref)
    return pl.pallas_call(
        _fox_cumgate_kernel,
        grid=(s // tt,),
        in_specs=[pl.BlockSpec((tt, LANES), lambda i: (i, 0)),
                  pl.BlockSpec((1, LANES), lambda i: (0, 0))],
        out_specs=pl.BlockSpec((tt, LANES), lambda i: (i, 0)),
        out_shape=jax.ShapeDtypeStruct((s, LANES), F32),
        scratch_shapes=[pltpu.VMEM((1, LANES), F32)],
        compiler_params=_params(("arbitrary",)),
        name="fox_cumulative_gate",
    )(f_logits, b_row)


def _fox_attn_kernel(q_ref, k_ref, vt_ref, gate_ref, o_ref, m_ref, l_ref, acc_ref):
    qi = pl.program_id(1)
    tq = q_ref.shape[0]
    tk = tq
    q = q_ref[...]

    m_ref[...] = jnp.full_like(m_ref, NEG_BIG)
    l_ref[...] = jnp.zeros_like(l_ref)
    acc_ref[...] = jnp.zeros_like(acc_ref)

    def block(kj, masked, state):
        m_old, l_old, acc_old = state
        k0 = pl.multiple_of(kj * tk, tk)
        kb = k_ref[pl.ds(k0, tk), :]
        st = lax.dot_general(kb, q, (((1,), (1,)), ((), ())), preferred_element_type=F32)
        if masked:
            kpos = lax.broadcasted_iota(I32, (tk, tq), 0)
            qpos = lax.broadcasted_iota(I32, (tk, tq), 1)
            st = jnp.where(kpos <= qpos, st, -jnp.inf)
        m_new = jnp.maximum(m_old, jnp.max(st, axis=0, keepdims=True))
        alpha = jnp.exp2(m_old - m_new)
        p = jnp.exp2(st - m_new)
        l_new = alpha * l_old + jnp.sum(p, axis=0, keepdims=True)
        vb = vt_ref[:, pl.ds(k0, tk)]
        acc_new = alpha * acc_old + jnp.dot(vb, p.astype(BF16), preferred_element_type=F32)
        return m_new, l_new, acc_new

    def load_state():
        return m_ref[...], l_ref[...], acc_ref[...]

    def store_state(state):
        m_ref[...], l_ref[...], acc_ref[...] = state

    def full_body(kj, carry):
        store_state(block(kj, False, load_state()))
        return carry

    lax.fori_loop(0, qi, full_body, 0)
    _, l_fin, acc_fin = block(qi, True, load_state())

    o_t = acc_fin / l_fin
    o_ref[...] = (o_t.T * _sigmoid(gate_ref[...])).astype(BF16)


def fox_attention(qa, ka, v_t, proj, n_heads, *, tq_pref=1024):
    s = qa.shape[0]
    d = HEAD_DIM
    tq = _tile(s, tq_pref)
    return pl.pallas_call(
        _fox_attn_kernel,
        grid=(n_heads, s // tq),
        in_specs=[pl.BlockSpec((tq, 2 * d), lambda h, i: (i, h)),
                  pl.BlockSpec((s, 2 * d), lambda h, i: (0, h)),
                  pl.BlockSpec((d, s), lambda h, i: (h, 0)),
                  pl.BlockSpec((tq, d), lambda h, i: (i, 3 * n_heads + h))],
        out_specs=pl.BlockSpec((tq, d), lambda h, i: (i, h)),
        out_shape=jax.ShapeDtypeStruct((s, n_heads * d), BF16),
        scratch_shapes=[pltpu.VMEM((1, tq), F32), pltpu.VMEM((1, tq), F32), pltpu.VMEM((d, tq), F32)],
        compiler_params=_params(("parallel", "arbitrary")),
        name="fox_attention",
    )(qa, ka, v_t, proj)


def _split_bf16(x):
    hi = x.astype(BF16)
    lo = (x - hi.astype(F32)).astype(BF16)
    return hi, lo


def _first_argmax(vals, mask, lane_f):
    v = jnp.where(mask, vals, NEG_BIG)
    vmax = jnp.max(v, axis=1, keepdims=True)
    hit = jnp.logical_and(mask, v == vmax)
    idx = jnp.min(jnp.where(hit, lane_f, float(LANES)), axis=1, keepdims=True)
    return vmax, idx


def _router_kernel(x_ref, w_ref, b_ref, ids_ref, wts_ref, rank_ref, cnt_ref, carry_ref, *, n_groups, n_exp):
    i = pl.program_id(0)
    tm = x_ref.shape[0]

    @pl.when(i == 0)
    def _():
        carry_ref[...] = jnp.zeros_like(carry_ref)

    xh, xl = _split_bf16(x_ref[...])
    wh, wl = _split_bf16(w_ref[...])
    lg = (jnp.dot(xh, wh, preferred_element_type=F32) + jnp.dot(xh, wl, preferred_element_type=F32)
          + jnp.dot(xl, wh, preferred_element_type=F32)) + b_ref[...]
    lane = lax.broadcasted_iota(I32, lg.shape, 1)
    lane_f = lane.astype(F32)

    gmask = lane < n_groups
    gmax, gidx = _first_argmax(lg, gmask, lane_f)
    gsum = jnp.sum(jnp.where(gmask, jnp.exp(jnp.where(gmask, lg, NEG_BIG) - gmax), 0.0), axis=1, keepdims=True)
    p_group = 1.0 / gsum

    e_lo = float(n_groups) + gidx * float(n_exp)
    emask = jnp.logical_and(lane_f >= e_lo, lane_f < e_lo + float(n_exp))
    t1, i1 = _first_argmax(lg, emask, lane_f)
    emask2 = jnp.logical_and(emask, lane_f != i1)
    t2, i2 = _first_argmax(lg, emask2, lane_f)
    e21 = jnp.exp(t2 - t1)
    p1 = p_group / (1.0 + e21)
    p2 = p_group * e21 / (1.0 + e21)
    id1 = i1 - float(n_groups)
    id2 = i2 - float(n_groups)

    oh = jnp.where(lane_f == id1, 1.0, 0.0) + jnp.where(lane_f == id2, 1.0, 0.0)
    r = lax.broadcasted_iota(I32, (tm, tm), 0)
    cc = lax.broadcasted_iota(I32, (tm, tm), 1)
    tril = jnp.where(r > cc, 1.0, 0.0).astype(BF16)
    before = jnp.dot(tril, oh.astype(BF16), preferred_element_type=F32) + carry_ref[...]
    rank1 = jnp.sum(jnp.where(lane_f == id1, before, 0.0), axis=1, keepdims=True)
    rank2 = jnp.sum(jnp.where(lane_f == id2, before, 0.0), axis=1, keepdims=True)
    new_carry = carry_ref[...] + jnp.sum(oh, axis=0, keepdims=True)
    carry_ref[...] = new_carry
    cnt_ref[...] = jnp.broadcast_to(new_carry, cnt_ref.shape).astype(I32)

    first = lane == 0
    second = lane == 1
    ids_ref[...] = jnp.where(first, id1, jnp.where(second, id2, 0.0)).astype(I32)
    wts_ref[...] = jnp.where(first, p1, jnp.where(second, p2, 0.0))
    rank_ref[...] = jnp.where(first, rank1, jnp.where(second, rank2, 0.0)).astype(I32)


def moe_router(x, w_pad, b_pad, n_groups, n_exp, *, tm_pref=512):
    s, dm = x.shape
    tm = _tile(s, tm_pref)
    return pl.pallas_call(
        functools.partial(_router_kernel, n_groups=n_groups, n_exp=n_exp),
        grid=(s // tm,),
        in_specs=[pl.BlockSpec((tm, dm), lambda i: (i, 0)),
                  pl.BlockSpec((dm, LANES), lambda i: (0, 0)),
                  pl.BlockSpec((1, LANES), lambda i: (0, 0))],
        out_specs=[pl.BlockSpec((tm, LANES), lambda i: (i, 0)),
                   pl.BlockSpec((tm, LANES), lambda i: (i, 0)),
                   pl.BlockSpec((tm, LANES), lambda i: (i, 0)),
                   pl.BlockSpec((SUBLANES, LANES), lambda i: (0, 0))],
        out_shape=[jax.ShapeDtypeStruct((s, LANES), I32), jax.ShapeDtypeStruct((s, LANES), F32),
                   jax.ShapeDtypeStruct((s, LANES), I32), jax.ShapeDtypeStruct((SUBLANES, LANES), I32)],
        scratch_shapes=[pltpu.VMEM((1, LANES), F32)],
        compiler_params=_params(("arbitrary",)),
        name="moe_router",
    )(x, w_pad, b_pad)


ROW_DMA_PRIORITIES = (0, 1)


def _start_row_gather(src_hbm, dst, sem, idx_ref, base):
    for r in range(dst.shape[0]):
        tok = idx_ref[base + r]
        pltpu.make_async_copy(src_hbm.at[pl.ds(tok, 1), :], dst.at[pl.ds(r, 1), :], sem).start(
            priority=ROW_DMA_PRIORITIES[r % 2])


def _wait_row_gather(dst, sem):
    pltpu.make_async_copy(dst, dst, sem).wait()


def _moe_dispatch_kernel(pos_ref, ends_ref, x_ref, xs_hbm, zbuf, zsem, sem):
    i = pl.program_id(0)
    tm = x_ref.shape[0]
    n_exp = ends_ref.shape[0]

    @pl.when(i == 0)
    def _():
        zbuf[...] = jnp.zeros_like(zbuf)

        def last_tile_copy(e):
            start = pl.multiple_of(ends_ref[e] - tm, tm)
            return pltpu.make_async_copy(zbuf, xs_hbm.at[pl.ds(start, tm), :], zsem)

        def non_empty(e):
            return ends_ref[e] > (ends_ref[e - 1] if e > 0 else 0)

        def tail_tile_copy(t):
            return pltpu.make_async_copy(zbuf, xs_hbm.at[pl.ds(pl.multiple_of(t * tm, tm), tm), :], zsem)

        n_used = ends_ref[n_exp - 1] // tm
        n_tiles = xs_hbm.shape[0] // tm
        for e in range(n_exp):
            pl.when(non_empty(e))(lambda e=e: last_tile_copy(e).start())
        lax.fori_loop(n_used, n_tiles, lambda t, c: (tail_tile_copy(t).start(), c)[1], 0)
        for e in range(n_exp):
            pl.when(non_empty(e))(lambda e=e: last_tile_copy(e).wait())
        lax.fori_loop(n_used, n_tiles, lambda t, c: (tail_tile_copy(t).wait(), c)[1], 0)

    for r in range(tm):
        for kk in range(MOE_TOP_K):
            slot = pos_ref[(i * tm + r) * MOE_TOP_K + kk]
            pltpu.make_async_copy(x_ref.at[pl.ds(r, 1), :], xs_hbm.at[pl.ds(slot, 1), :], sem).start(
                priority=ROW_DMA_PRIORITIES[kk])
    for kk in range(MOE_TOP_K):
        pltpu.make_async_copy(x_ref, x_ref, sem).wait()


def moe_dispatch(x, pos_flat, ends, n_tiles, tm):
    s, dm = x.shape
    grid_spec = pltpu.PrefetchScalarGridSpec(
        num_scalar_prefetch=2,
        grid=(s // tm,),
        in_specs=[pl.BlockSpec((tm, dm), lambda i, pos, ends: (i, 0))],
        out_specs=pl.BlockSpec(memory_space=pl.ANY),
        scratch_shapes=[pltpu.VMEM((tm, dm), F32),
                        pltpu.SemaphoreType.DMA(()),
                        pltpu.SemaphoreType.DMA(())])
    return pl.pallas_call(
        _moe_dispatch_kernel,
        grid_spec=grid_spec,
        out_shape=jax.ShapeDtypeStruct((n_tiles * tm, dm), F32),
        compiler_params=_params(("arbitrary",)),
        name="moe_dispatch",
    )(pos_flat, ends, x)


def _moe_ffn_kernel(te_ref, nu_ref, x_ref, wg_ref, wu_ref, wd_ref, o_ref, wgb, wub, wdb):
    i = pl.program_id(0)
    n_used = nu_ref[0]

    @pl.when(i < n_used)
    def _():
        is_new = jnp.logical_or(i == 0, te_ref[i] != te_ref[jnp.maximum(i - 1, 0)])

        @pl.when(is_new)
        def _():
            wgb[...] = wg_ref[...].astype(BF16)
            wub[...] = wu_ref[...].astype(BF16)
            wdb[...] = wd_ref[...].astype(BF16)

        x = x_ref[...].astype(BF16)
        gate = jnp.dot(x, wgb[...], preferred_element_type=F32)
        up = jnp.dot(x, wub[...], preferred_element_type=F32)
        hmid = (gate * _sigmoid(gate) * up).astype(BF16)
        o_ref[...] = jnp.dot(hmid, wdb[...], preferred_element_type=F32)

    @pl.when(i >= n_used)
    def _():
        o_ref[...] = jnp.zeros_like(o_ref)


def moe_expert_ffn(xs, w_gate, w_up, w_down, layer, tile_expert, n_used, tm):
    dm = xs.shape[1]
    n_exp_total = MOE_GROUPS * MOE_EXPERTS_PER_GROUP
    fe = w_gate.shape[-1]
    n_tiles = tile_expert.shape[0]
    wg = w_gate.reshape(-1, dm, fe)
    wu = w_up.reshape(-1, dm, fe)
    wd = w_down.reshape(-1, fe, dm)
    base = layer * n_exp_total
    grid_spec = pltpu.PrefetchScalarGridSpec(
        num_scalar_prefetch=2,
        grid=(n_tiles,),
        in_specs=[pl.BlockSpec((tm, dm), lambda i, te, nu: (jnp.minimum(i, nu[0] - 1), 0)),
                  pl.BlockSpec((None, dm, fe), lambda i, te, nu: (base + te[i], 0, 0)),
                  pl.BlockSpec((None, dm, fe), lambda i, te, nu: (base + te[i], 0, 0)),
                  pl.BlockSpec((None, fe, dm), lambda i, te, nu: (base + te[i], 0, 0))],
        out_specs=pl.BlockSpec((tm, dm), lambda i, te, nu: (i, 0)),
        scratch_shapes=[pltpu.VMEM((dm, fe), BF16),
                        pltpu.VMEM((dm, fe), BF16),
                        pltpu.VMEM((fe, dm), BF16)])
    return pl.pallas_call(
        _moe_ffn_kernel,
        grid_spec=grid_spec,
        out_shape=jax.ShapeDtypeStruct((n_tiles * tm, dm), F32),
        compiler_params=_params(("arbitrary",)),
        name="moe_expert_ffn",
    )(tile_expert, n_used, xs, wg, wu, wd)


def _moe_combine_kernel(pos_ref, y_hbm, x_ref, wts_ref, g_ref, b_ref, o_ref, ob_ref, ybuf, sem, *, alpha):
    i = pl.program_id(0)
    n = pl.num_programs(0)
    tm = x_ref.shape[0]

    def gather(tile, sl):
        for kk in range(MOE_TOP_K):
            _start_row_gather(y_hbm, ybuf.at[sl, kk], sem.at[sl], pos_ref, (kk * n + tile) * tm)

    @pl.when(i == 0)
    def _():
        gather(0, 0)

    def tile_body(slot, prefetch):
        _wait_row_gather(ybuf.at[slot], sem.at[slot])
        if prefetch:
            gather(i + 1, 1 - slot)
        wts = wts_ref[...]
        mix = wts[:, 0:1] * ybuf[slot, 0] + wts[:, 1:2] * ybuf[slot, 1]
        y = alpha * x_ref[...] + mix
        out = _layernorm_rows(y, g_ref[...], b_ref[...])
        o_ref[...] = out
        ob_ref[...] = out.astype(BF16)

    for slot in range(2):
        pl.when(jnp.logical_and(i + 1 < n, i % 2 == slot))(functools.partial(tile_body, slot, True))
        pl.when(jnp.logical_and(i + 1 == n, i % 2 == slot))(functools.partial(tile_body, slot, False))


def moe_combine_layernorm(y_sorted, pos_km, x, wts, g, b, alpha, *, tm_pref=256):
    s, dm = x.shape
    tm = _tile(s, tm_pref)
    grid_spec = pltpu.PrefetchScalarGridSpec(
        num_scalar_prefetch=1,
        grid=(s // tm,),
        in_specs=[pl.BlockSpec(memory_space=pl.ANY),
                  pl.BlockSpec((tm, dm), lambda i, pos: (i, 0)),
                  pl.BlockSpec((tm, LANES), lambda i, pos: (i, 0)),
                  pl.BlockSpec((1, dm), lambda i, pos: (0, 0)),
                  pl.BlockSpec((1, dm), lambda i, pos: (0, 0))],
        out_specs=[pl.BlockSpec((tm, dm), lambda i, pos: (i, 0)),
                   pl.BlockSpec((tm, dm), lambda i, pos: (i, 0))],
        scratch_shapes=[pltpu.VMEM((2, MOE_TOP_K, tm, dm), F32),
                        pltpu.SemaphoreType.DMA((2,))])
    return pl.pallas_call(
        functools.partial(_moe_combine_kernel, alpha=alpha),
        grid_spec=grid_spec,
        out_shape=[jax.ShapeDtypeStruct((s, dm), F32), jax.ShapeDtypeStruct((s, dm), BF16)],
        compiler_params=_params(("arbitrary",)),
        name="moe_combine_ln",
    )(pos_km, y_sorted, x, wts, g.reshape(1, dm), b.reshape(1, dm))


def _pad_cols(w, width=LANES):
    return jnp.pad(w, ((0, 0), (0, width - w.shape[1])))


def _moe_layer(x, x_bf16, layer, w_rg, b_rg, w_re, b_re, w_gate, w_up, w_down, ln_g, ln_b, alpha, *, tm=256):
    del x_bf16
    s, dm = x.shape
    n_groups, _, n_exp = w_re.shape
    n_total = n_groups * n_exp
    w_r = _pad_cols(jnp.concatenate([w_rg, jnp.transpose(w_re, (1, 0, 2)).reshape(dm, n_total)], axis=1))
    b_r = _pad_cols(jnp.concatenate([b_rg, b_re.reshape(n_total)])[None, :])
    ids, wts, ranks, counts = moe_router(x, w_r, b_r, n_groups, n_exp)

    n_tiles = (s * MOE_TOP_K) // tm + n_total
    counts = counts[0, :n_total]
    padded = ((counts + tm - 1) // tm) * tm
    ends = jnp.cumsum(padded).astype(I32)
    starts = ends - padded
    ids2 = ids[:, :MOE_TOP_K]
    start_of_pick = jnp.sum(jnp.where(ids2[..., None] == jnp.arange(n_total, dtype=I32), starts, 0), axis=-1)
    pos = (start_of_pick + ranks[:, :MOE_TOP_K]).astype(I32)
    n_used = (ends[-1] // tm).astype(I32)
    tile_start = jnp.arange(n_tiles, dtype=I32) * tm
    tile_expert = jnp.minimum(jnp.searchsorted(ends, tile_start, side="right"), n_total - 1).astype(I32)
    last_expert = tile_expert[jnp.maximum(n_used - 1, 0)]
    tile_expert = jnp.where(jnp.arange(n_tiles) < n_used, tile_expert, last_expert)

    xs = moe_dispatch(x, pos.reshape(-1), ends, n_tiles, tm)
    y_sorted = moe_expert_ffn(xs, w_gate, w_up, w_down, layer, tile_expert, n_used.reshape(1), tm)
    pos_km = jnp.transpose(pos.reshape(s // tm, tm, MOE_TOP_K), (2, 0, 1)).reshape(-1).astype(I32)
    return moe_combine_layernorm(y_sorted, pos_km, x, wts, ln_g, ln_b, alpha, tm_pref=tm)


def kernel(x, dn_w_in, dn_conv_w, dn_A_log, dn_dt_bias, dn_norm_w, dn_w_out, fox_w_in, fox_b_f, fox_q_norm_w,
           fox_k_norm_w, fox_w_out, ln1_g, ln1_b, ln2_g, ln2_b, moe_w_rg, moe_b_rg, moe_w_re, moe_b_re,
           moe_w_gate, moe_w_up, moe_w_down):
    batch, s, dm = x.shape
    assert batch == 1
    depth = ln1_g.shape[0]
    alpha = (2 * depth) ** 0.25
    x2d = x.reshape(s, dm)
    xb = x2d
    d = HEAD_DIM
    hb = 8

    for i in range(depth):
        j = i // 2
        if i % 2 == 0:
            n_v_heads = dn_A_log.shape[1]
            qk_dim = (n_v_heads // 2) * d
            v_dim = n_v_heads * d
            conv_dim = 2 * qk_dim + v_dim
            w_in_t = jnp.transpose(dn_w_in[j])
            proj = matmul(xb, w_in_t, conv_dim + v_dim)
            hg = n_v_heads // hb
            w_b = w_in_t[conv_dim + v_dim:conv_dim + v_dim + n_v_heads].reshape(hg, hb, dm)
            w_a = w_in_t[conv_dim + v_dim + n_v_heads:].reshape(hg, hb, dm)
            w_ba = jnp.pad(jnp.concatenate([w_b, w_a], axis=1), ((0, 0), (0, LANES - 2 * hb), (0, 0)))
            ba_g = matmul(xb, w_ba.reshape(hg * LANES, dm), hg * LANES)
            zeros = jnp.zeros((hg, hb), F32)
            alog_g = jnp.pad(jnp.concatenate([zeros, dn_A_log[j].reshape(hg, hb)], axis=1),
                             ((0, 0), (0, LANES - 2 * hb))).reshape(hg, 1, LANES)
            dt_g = jnp.pad(jnp.concatenate([zeros, dn_dt_bias[j].reshape(hg, hb)], axis=1),
                           ((0, 0), (0, LANES - 2 * hb))).reshape(hg, 1, LANES)
            qkv = gdn_preprocess(proj, dn_conv_w[j], qk_dim)
            mixed = gdn_delta_rule(qkv, proj, ba_g, alog_g, dt_g, dn_norm_w[j], n_v_heads, hb=hb)
            w_out = dn_w_out[j]
        else:
            n_heads = fox_b_f.shape[1]
            dim = n_heads * d
            w_in_t = jnp.transpose(fox_w_in[j])
            proj = matmul(xb, w_in_t, 4 * dim)
            w_f = jnp.pad(w_in_t[4 * dim:], ((0, LANES - n_heads), (0, 0)))
            f_logits = matmul(xb, w_f, LANES, tn_pref=LANES)
            c_pad = fox_cumulative_gate(f_logits, _pad_cols(fox_b_f[j][None, :]))
            qa, ka, v_t = fox_preprocess(proj, c_pad, fox_q_norm_w[j], fox_k_norm_w[j], n_heads)
            mixed = fox_attention(qa, ka, v_t, proj, n_heads)
            w_out = fox_w_out[j]
        x2d, xb = matmul_residual_layernorm(mixed, w_out.astype(BF16), x2d, ln1_g[i], ln1_b[i], alpha)
        x2d, xb = _moe_layer(x2d, xb, i, moe_w_rg[i], moe_b_rg[i], moe_w_re[i], moe_b_re[i],
                             moe_w_gate, moe_w_up, moe_w_down, ln2_g[i], ln2_b[i], alpha)
    return x2d.reshape(batch, s, dm)
```

```python
import functools
import math

import jax
import jax.numpy as jnp
from jax import lax
from jax.experimental import pallas as pl
from jax.experimental.pallas import tpu as pltpu

F32 = jnp.float32
BF16 = jnp.bfloat16
I32 = jnp.int32

LANES = 128
SUBLANES = 8
V7X_VMEM_BYTES = 64 * 1024 * 1024
VMEM_LIMIT = V7X_VMEM_BYTES - 8 * 1024 * 1024

HEAD_DIM = 128
SOLVE_CHUNK = 128
DN_CONV_WIDTH = 4
MOE_GROUPS = 4
MOE_EXPERTS_PER_GROUP = 8
MOE_TOP_K = 2
NORM_EPS = 1e-6
NEG_BIG = -1e30
LOG2E = 1.4426950408889634


def _params(sem, vmem=VMEM_LIMIT):
    return pltpu.CompilerParams(dimension_semantics=sem, vmem_limit_bytes=vmem)


def _tile(dim, pref):
    t = min(dim, pref)
    assert dim % t == 0, (dim, pref)
    return t


def _softplus(x):
    return jnp.maximum(x, 0.0) + jnp.log(1.0 + jnp.exp(-jnp.abs(x)))


def _sigmoid(x):
    return 1.0 / (1.0 + jnp.exp(-x))


def _mm_kernel(a_ref, wt_ref, o_ref):
    o_ref[...] = lax.dot_general(a_ref[...], wt_ref[...].astype(BF16), (((1,), (1,)), ((), ())),
                                 preferred_element_type=F32)


def matmul(a, w_t, n_out, *, tm_pref=2048, tn_pref=512):
    m, k = a.shape
    assert a.dtype == BF16
    tm = _tile(m, tm_pref)
    tn = _tile(n_out, tn_pref)
    return pl.pallas_call(
        _mm_kernel,
        grid=(m // tm, n_out // tn),
        in_specs=[pl.BlockSpec((tm, k), lambda i, j: (i, 0)),
                  pl.BlockSpec((tn, k), lambda i, j: (j, 0))],
        out_specs=pl.BlockSpec((tm, tn), lambda i, j: (i, j)),
        out_shape=jax.ShapeDtypeStruct((m, n_out), F32),
        compiler_params=_params(("parallel", "parallel")),
        name="proj_matmul",
    )(a, w_t)


def _layernorm_rows(y, g, b):
    mu = jnp.mean(y, axis=1, keepdims=True)
    yc = y - mu
    var = jnp.mean(yc * yc, axis=1, keepdims=True)
    return yc * lax.rsqrt(var + NORM_EPS) * g + b


def _mm_res_ln_kernel(a_ref, w_ref, res_ref, g_ref, b_ref, o_ref, ob_ref, *, alpha):
    y = alpha * res_ref[...] + jnp.dot(a_ref[...], w_ref[...], preferred_element_type=F32)
    out = _layernorm_rows(y, g_ref[...], b_ref[...])
    o_ref[...] = out
    ob_ref[...] = out.astype(BF16)


def matmul_residual_layernorm(a, w_bf16, res, g, b, alpha, *, tm_pref=512):
    m, k = a.shape
    n = w_bf16.shape[1]
    tm = _tile(m, tm_pref)
    return pl.pallas_call(
        functools.partial(_mm_res_ln_kernel, alpha=alpha),
        grid=(m // tm,),
        in_specs=[pl.BlockSpec((tm, k), lambda i: (i, 0)),
                  pl.BlockSpec((k, n), lambda i: (0, 0), pipeline_mode=pl.Buffered(1)),
                  pl.BlockSpec((tm, n), lambda i: (i, 0)),
                  pl.BlockSpec((1, n), lambda i: (0, 0)),
                  pl.BlockSpec((1, n), lambda i: (0, 0))],
        out_specs=[pl.BlockSpec((tm, n), lambda i: (i, 0)),
                   pl.BlockSpec((tm, n), lambda i: (i, 0))],
        out_shape=[jax.ShapeDtypeStruct((m, n), F32), jax.ShapeDtypeStruct((m, n), BF16)],
        compiler_params=_params(("parallel",)),
        name="out_proj_res_ln",
    )(a, w_bf16, res, g.reshape(1, n), b.reshape(1, n))


def _gdn_pre_kernel(cur_ref, halo_ref, w_ref, o_ref, ext_ref, *, n_qk_blocks, q_scale):
    i = pl.program_id(0)
    j = pl.program_id(1)
    tt, cb = cur_ref.shape
    keep = (i > 0).astype(F32)
    ext_ref[0:SUBLANES, :] = halo_ref[...] * keep
    ext_ref[SUBLANES:SUBLANES + tt, :] = cur_ref[...]
    w = w_ref[...]
    acc = cur_ref[...] * w[DN_CONV_WIDTH - 1:DN_CONV_WIDTH]
    for kk in range(DN_CONV_WIDTH - 1):
        off = SUBLANES - (DN_CONV_WIDTH - 1) + kk
        acc = acc + ext_ref[off:off + tt, :] * w[kk:kk + 1]
    y = acc * _sigmoid(acc)

    def write_normed(scale):
        for h in range(cb // HEAD_DIM):
            yh = y[:, h * HEAD_DIM:(h + 1) * HEAD_DIM]
            r = lax.rsqrt(jnp.sum(yh * yh, axis=1, keepdims=True) + NORM_EPS)
            o_ref[:, h * HEAD_DIM:(h + 1) * HEAD_DIM] = yh * (r * scale if scale != 1.0 else r)

    @pl.when(j < n_qk_blocks)
    def _():
        write_normed(q_scale)

    @pl.when(jnp.logical_and(j >= n_qk_blocks, j < 2 * n_qk_blocks))
    def _():
        write_normed(1.0)

    @pl.when(j >= 2 * n_qk_blocks)
    def _():
        o_ref[...] = y


def gdn_preprocess(proj, conv_w, qk_dim, *, tt_pref=512, cb_pref=512):
    s = proj.shape[0]
    conv_dim = conv_w.shape[1]
    tt = _tile(s, tt_pref)
    cb = _tile(qk_dim, cb_pref)
    assert conv_dim % cb == 0 and tt % SUBLANES == 0
    hb = tt // SUBLANES
    return pl.pallas_call(
        functools.partial(_gdn_pre_kernel, n_qk_blocks=qk_dim // cb, q_scale=HEAD_DIM ** -0.5),
        grid=(s // tt, conv_dim // cb),
        in_specs=[pl.BlockSpec((tt, cb), lambda i, j: (i, j)),
                  pl.BlockSpec((SUBLANES, cb), lambda i, j: (jnp.maximum(i * hb - 1, 0), j)),
                  pl.BlockSpec((DN_CONV_WIDTH, cb), lambda i, j: (0, j))],
        out_specs=pl.BlockSpec((tt, cb), lambda i, j: (i, j)),
        out_shape=jax.ShapeDtypeStruct((s, conv_dim), F32),
        scratch_shapes=[pltpu.VMEM((tt + SUBLANES, cb), F32)],
        compiler_params=_params(("parallel", "parallel")),
        name="gdn_conv_norm",
    )(proj, proj, conv_w)


def _bmm(a, b):
    return jnp.einsum("nij,njk->nik", a.astype(BF16), b.astype(BF16), preferred_element_type=F32)


def _bmm_nt(a, b):
    return jnp.einsum("nid,njd->nij", a.astype(BF16), b.astype(BF16), preferred_element_type=F32)


def _chunk_cumsum(x, chunk, reverse=False):
    rows = x.shape[0]
    pos = lax.broadcasted_iota(I32, x.shape, 0) % chunk
    step = 1
    while step < chunk:
        if reverse:
            shifted = pltpu.roll(x, rows - step, axis=0)
            x = x + jnp.where(pos + step < chunk, shifted, 0.0)
        else:
            shifted = pltpu.roll(x, step, axis=0)
            x = x + jnp.where(pos >= step, shifted, 0.0)
        step *= 2
    return x


def _cat_lanes(a, b):
    return jnp.concatenate([a, b], axis=-1)


def _block_diag(a, b):
    return jnp.concatenate([_cat_lanes(a, jnp.zeros_like(b)), _cat_lanes(jnp.zeros_like(a), b)], axis=-2)


def _unit_lower_inverse_pair(lower_a, lower_b, eye):
    c = lower_a.shape[-1]
    pa, pb = lower_a.astype(BF16), lower_b.astype(BF16)
    tinv = _cat_lanes(eye - lower_a, eye - lower_b)
    span = 2
    while span < c:
        power = jnp.einsum("nij,njk->nik", _cat_lanes(pa, pb), _block_diag(pa, pb), preferred_element_type=F32)
        pa, pb = power[..., :c].astype(BF16), power[..., c:].astype(BF16)
        tinv = tinv + jnp.einsum("nij,njk->nik", tinv.astype(BF16), _block_diag(pa, pb),
                                 preferred_element_type=F32)
        span *= 2
    return tinv[..., :c], tinv[..., c:]


def _delta_kernel(q_ref, k_ref, v_ref, z_ref, ba_ref, alog_ref, dt_ref, nw_ref, o_ref,
                  state_ref, u_ref, wq_ref, ak_ref, egl_ref, osc_ref, *, hb):
    t = pl.program_id(1)
    tb = q_ref.shape[0]
    c = SOLVE_CHUNK
    nc = tb // c
    d = HEAD_DIM

    @pl.when(t == 0)
    def _():
        state_ref[...] = jnp.zeros_like(state_ref)

    ba = ba_ref[...]
    beta_all = _sigmoid(ba)
    g_all = -jnp.exp(alog_ref[...]) * _softplus(ba + dt_ref[...])
    big_g = _chunk_cumsum(g_all, c)
    tail_g = _chunk_cumsum(g_all, c, reverse=True) - g_all
    e_g = jnp.exp(big_g)
    e_tail = jnp.exp(tail_g)
    e_tot = jnp.exp(big_g + tail_g)
    big_g_t = big_g.T

    row = lax.broadcasted_iota(I32, (c, c), 0)
    col = lax.broadcasted_iota(I32, (c, c), 1)
    causal = row >= col
    strict = row > col
    eye = (row == col).astype(F32)[None]

    def head_terms(hh):
        gi = hb + hh
        gcol_b = jnp.broadcast_to(big_g[:, gi:gi + 1], (tb, c))
        decays = []
        for n in range(nc):
            diff = gcol_b[n * c:(n + 1) * c, :] - big_g_t[gi:gi + 1, n * c:(n + 1) * c]
            decays.append(jnp.where(causal, jnp.exp(jnp.where(causal, diff, 0.0)), 0.0))
        return beta_all[:, hh:hh + 1], e_g[:, gi:gi + 1], e_tail[:, gi:gi + 1], jnp.stack(decays, axis=0)

    for p in range(hb // 2):
        heads = (2 * p, 2 * p + 1)
        qh = q_ref[:, p * d:(p + 1) * d]
        kh = k_ref[:, p * d:(p + 1) * d]
        terms = [head_terms(hh) for hh in heads]
        kbs = [kh * tm_[0] for tm_ in terms]
        stacked = jnp.concatenate([kbs[0].reshape(nc, c, d), kbs[1].reshape(nc, c, d), qh.reshape(nc, c, d)],
                                  axis=1)
        prod = _bmm_nt(stacked, kh.reshape(nc, c, d))
        qk = prod[:, 2 * c:]
        lowers = [jnp.where(strict[None], prod[:, i * c:(i + 1) * c] * terms[i][3], 0.0) for i in range(2)]
        tinvs = _unit_lower_inverse_pair(lowers[0], lowers[1], eye)
        packed = []
        for i, hh in enumerate(heads):
            bcol, egcol, etcol, decay = terms[i]
            vh = v_ref[:, hh * d:(hh + 1) * d]
            rhs = _cat_lanes((vh * bcol).reshape(nc, c, d), (kbs[i] * egcol).reshape(nc, c, d))
            sol = _bmm(tinvs[i], rhs)
            u_ref[hh] = sol[..., :d]
            wq_i = jnp.concatenate([sol[..., d:], (qh * egcol).reshape(nc, c, d)], axis=1).astype(BF16)
            kd_t = (kh * etcol).T.astype(BF16)
            kdt3 = jnp.stack([kd_t[:, n * c:(n + 1) * c] for n in range(nc)], axis=0)
            packed.append((wq_i, jnp.concatenate([(qk * decay).astype(BF16), kdt3], axis=1)))
            egl_ref[hh] = jnp.broadcast_to(e_tot[:, hb + hh:hb + hh + 1], (tb, d))
        wq_ref[p] = _cat_lanes(packed[0][0], packed[1][0])
        ak_ref[p] = _cat_lanes(packed[0][1], packed[1][1])

    states = [state_ref[hh] for hh in range(hb)]
    for n in range(nc):
        r0 = n * c
        for p in range(hb // 2):
            ha, hb_ = 2 * p, 2 * p + 1
            sa, sb = states[ha], states[hb_]
            ws = jnp.dot(wq_ref[p, n], _block_diag(sa.astype(BF16), sb.astype(BF16)),
                         preferred_element_type=F32)
            vna = u_ref[ha, n] - ws[:c, :d]
            vnb = u_ref[hb_, n] - ws[:c, d:]
            r = jnp.dot(ak_ref[p, n], _block_diag(vna.astype(BF16), vnb.astype(BF16)),
                        preferred_element_type=F32)
            osc_ref[r0:r0 + c, ha * d:(ha + 1) * d] = ws[c:, :d] + r[:c, :d]
            osc_ref[r0:r0 + c, hb_ * d:(hb_ + 1) * d] = ws[c:, d:] + r[:c, d:]
            states[ha] = sa * egl_ref[ha, r0:r0 + 1, :] + r[c:, :d]
            states[hb_] = sb * egl_ref[hb_, r0:r0 + 1, :] + r[c:, d:]
    for hh in range(hb):
        state_ref[hh] = states[hh]

    nw = nw_ref[...]
    for hh in range(hb):
        o = osc_ref[:, hh * d:(hh + 1) * d]
        z = z_ref[:, hh * d:(hh + 1) * d]
        ms = jnp.mean(o * o, axis=1, keepdims=True)
        y = o * lax.rsqrt(ms + NORM_EPS) * nw * (z * _sigmoid(z))
        o_ref[:, hh * d:(hh + 1) * d] = y.astype(BF16)


def gdn_delta_rule(qkv, proj, ba_g, alog_g, dt_g, norm_w, n_v_heads, *, hb, tb_pref=1024):
    s = qkv.shape[0]
    d = HEAD_DIM
    c = SOLVE_CHUNK
    qk_dim = (n_v_heads // 2) * d
    v_dim = n_v_heads * d
    tb = _tile(s, tb_pref)
    nc = tb // c
    hg = n_v_heads // hb
    qw = (hb // 2) * d
    vw = hb * d
    return pl.pallas_call(
        functools.partial(_delta_kernel, hb=hb),
        grid=(hg, s // tb),
        in_specs=[pl.BlockSpec((tb, qw), lambda g, t: (t, g)),
                  pl.BlockSpec((tb, qw), lambda g, t: (t, qk_dim // qw + g)),
                  pl.BlockSpec((tb, vw), lambda g, t: (t, 2 * qk_dim // vw + g)),
                  pl.BlockSpec((tb, vw), lambda g, t: (t, (2 * qk_dim + v_dim) // vw + g)),
                  pl.BlockSpec((tb, LANES), lambda g, t: (t, g)),
                  pl.BlockSpec((None, 1, LANES), lambda g, t: (g, 0, 0)),
                  pl.BlockSpec((None, 1, LANES), lambda g, t: (g, 0, 0)),
                  pl.BlockSpec((1, d), lambda g, t: (0, 0))],
        out_specs=pl.BlockSpec((tb, vw), lambda g, t: (t, g)),
        out_shape=jax.ShapeDtypeStruct((s, v_dim), BF16),
        scratch_shapes=[pltpu.VMEM((hb, d, d), F32),
                        pltpu.VMEM((hb, nc, c, d), F32),
                        pltpu.VMEM((hb // 2, nc, 2 * c, 2 * d), BF16),
                        pltpu.VMEM((hb // 2, nc, c + d, 2 * c), BF16),
                        pltpu.VMEM((hb, tb, d), F32),
                        pltpu.VMEM((tb, vw), F32)],
        compiler_params=_params(("parallel", "arbitrary")),
        name="gdn_delta_rule",
    )(qkv, qkv, qkv, proj, ba_g, alog_g, dt_g, norm_w.reshape(1, d))


def _split3_bf16(x):
    hi = x.astype(BF16).astype(F32)
    r1 = x - hi
    mid = r1.astype(BF16).astype(F32)
    return hi, mid, r1 - mid


def _fox_pre_kernel(q_ref, k_ref, v_ref, c_ref, qw_ref, kw_ref, qa_ref, ka_ref, vt_ref, *, n_heads):
    d = HEAD_DIM
    tt = q_ref.shape[0]
    c2 = c_ref[...] * LOG2E
    lane = lax.broadcasted_iota(I32, (tt, d), 1)
    qw = qw_ref[...] * (d ** -0.5 * LOG2E)
    kw = kw_ref[...]
    for h in range(n_heads):
        qh = q_ref[:, h * d:(h + 1) * d]
        kh = k_ref[:, h * d:(h + 1) * d]
        qn = qh * lax.rsqrt(jnp.mean(qh * qh, axis=1, keepdims=True) + NORM_EPS) * qw
        kn = kh * lax.rsqrt(jnp.mean(kh * kh, axis=1, keepdims=True) + NORM_EPS) * kw
        hi, mid, lo = _split3_bf16(c2[:, h:h + 1])
        aq = jnp.where(lane == 0, hi, jnp.where(lane == 1, mid, jnp.where(lane == 2, lo,
                                                                          jnp.where(lane < 6, 1.0, 0.0))))
        ak = jnp.where(lane < 3, 1.0, jnp.where(lane == 3, -hi, jnp.where(lane == 4, -mid,
                                                                           jnp.where(lane == 5, -lo, 0.0))))
        qa_ref[:, 2 * h * d:(2 * h + 1) * d] = qn.astype(BF16)
        qa_ref[:, (2 * h + 1) * d:(2 * h + 2) * d] = aq.astype(BF16)
        ka_ref[:, 2 * h * d:(2 * h + 1) * d] = kn.astype(BF16)
        ka_ref[:, (2 * h + 1) * d:(2 * h + 2) * d] = ak.astype(BF16)
    vt_ref[...] = v_ref[...].T.astype(BF16)


def fox_preprocess(proj, c_pad, q_norm_w, k_norm_w, n_heads, *, tt_pref=256):
    s = proj.shape[0]
    d = HEAD_DIM
    dim = n_heads * d
    tt = _tile(s, tt_pref)
    return pl.pallas_call(
        functools.partial(_fox_pre_kernel, n_heads=n_heads),
        grid=(s // tt,),
        in_specs=[pl.BlockSpec((tt, dim), lambda i: (i, 0)),
                  pl.BlockSpec((tt, dim), lambda i: (i, 1)),
                  pl.BlockSpec((tt, dim), lambda i: (i, 2)),
                  pl.BlockSpec((tt, LANES), lambda i: (i, 0)),
                  pl.BlockSpec((1, d), lambda i: (0, 0)),
                  pl.BlockSpec((1, d), lambda i: (0, 0))],
        out_specs=[pl.BlockSpec((tt, 2 * dim), lambda i: (i, 0)),
                   pl.BlockSpec((tt, 2 * dim), lambda i: (i, 0)),
                   pl.BlockSpec((dim, tt), lambda i: (0, i))],
        out_shape=[jax.ShapeDtypeStruct((s, 2 * dim), BF16), jax.ShapeDtypeStruct((s, 2 * dim), BF16),
                   jax.ShapeDtypeStruct((dim, s), BF16)],
        compiler_params=_params(("parallel",)),
        name="fox_qkv_prep",
    )(proj, proj, proj, c_pad, q_norm_w.reshape(1, d), k_norm_w.reshape(1, d))


def _fox_cumgate_kernel(f_ref, b_ref, o_ref, carry_ref):
    i = pl.program_id(0)
    tt = f_ref.shape[0]

    @pl.when(i == 0)
    def _():
        carry_ref[...] = jnp.zeros_like(carry_ref)

    x = f_ref[...] + b_ref[...]
    log_f = jnp.minimum(x, 0.0) - jnp.log(1.0 + jnp.exp(-jnp.abs(x)))
    csum = _chunk_cumsum(log_f, tt) + carry_ref[...]
    o_ref[...] = csum
    carry_ref[...] = csum[tt - 1:tt, :]


def fox_cumulative_gate(f_logits, b_row, *, tt_pref=512):
    s = f_logits.shape[0]
    tt = _tile(s, tt_pref)
    return pl.pallas_call(
        _fox_cumgate_kernel,
        grid=(s // tt,),
        in_specs=[pl.BlockSpec((tt, LANES), lambda i: (i, 0)),
                  pl.BlockSpec((1, LANES), lambda i: (0, 0))],
        out_specs=pl.BlockSpec((tt, LANES), lambda i: (i, 0)),
        out_shape=jax.ShapeDtypeStruct((s, LANES), F32),
        scratch_shapes=[pltpu.VMEM((1, LANES), F32)],
        compiler_params=_params(("arbitrary",)),
        name="fox_cumulative_gate",
    )(f_logits, b_row)


def _fox_attn_kernel(q_ref, k_ref, vt_ref, gate_ref, o_ref, m_ref, l_ref, acc_ref):
    qi = pl.program_id(1)
    tq = q_ref.shape[0]
    tk = tq
    q = q_ref[...]

    m_ref[...] = jnp.full_like(m_ref, NEG_BIG)
    l_ref[...] = jnp.zeros_like(l_ref)
    acc_ref[...] = jnp.zeros_like(acc_ref)

    def block(kj, masked, state):
        m_old, l_old, acc_old = state
        k0 = pl.multiple_of(kj * tk, tk)
        kb = k_ref[pl.ds(k0, tk), :]
        st = lax.dot_general(kb, q, (((1,), (1,)), ((), ())), preferred_element_type=F32)
        if masked:
            kpos = lax.broadcasted_iota(I32, (tk, tq), 0)
            qpos = lax.broadcasted_iota(I32, (tk, tq), 1)
            st = jnp.where(kpos <= qpos, st, -jnp.inf)
        m_new = jnp.maximum(m_old, jnp.max(st, axis=0, keepdims=True))
        alpha = jnp.exp2(m_old - m_new)
        p = jnp.exp2(st - m_new)
        l_new = alpha * l_old + jnp.sum(p, axis=0, keepdims=True)
        vb = vt_ref[:, pl.ds(k0, tk)]
        acc_new = alpha * acc_old + jnp.dot(vb, p.astype(BF16), preferred_element_type=F32)
        return m_new, l_new, acc_new

    def load_state():
        return m_ref[...], l_ref[...], acc_ref[...]

    def store_state(state):
        m_ref[...], l_ref[...], acc_ref[...] = state

    def full_body(kj, carry):
        store_state(block(kj, False, load_state()))
        return carry

    lax.fori_loop(0, qi, full_body, 0)
    _, l_fin, acc_fin = block(qi, True, load_state())

    o_t = acc_fin / l_fin
    o_ref[...] = (o_t.T * _sigmoid(gate_ref[...])).astype(BF16)


def fox_attention(qa, ka, v_t, proj, n_heads, *, tq_pref=1024):
    s = qa.shape[0]
    d = HEAD_DIM
    tq = _tile(s, tq_pref)
    return pl.pallas_call(
        _fox_attn_kernel,
        grid=(n_heads, s // tq),
        in_specs=[pl.BlockSpec((tq, 2 * d), lambda h, i: (i, h)),
                  pl.BlockSpec((s, 2 * d), lambda h, i: (0, h)),
                  pl.BlockSpec((d, s), lambda h, i: (h, 0)),
                  pl.BlockSpec((tq, d), lambda h, i: (i, 3 * n_heads + h))],
        out_specs=pl.BlockSpec((tq, d), lambda h, i: (i, h)),
        out_shape=jax.ShapeDtypeStruct((s, n_heads * d), BF16),
        scratch_shapes=[pltpu.VMEM((1, tq), F32), pltpu.VMEM((1, tq), F32), pltpu.VMEM((d, tq), F32)],
        compiler_params=_params(("parallel", "arbitrary")),
        name="fox_attention",
    )(qa, ka, v_t, proj)


def _split_bf16(x):
    hi = x.astype(BF16)
    lo = (x - hi.astype(F32)).astype(BF16)
    return hi, lo


def _first_argmax(vals, mask, lane_f):
    v = jnp.where(mask, vals, NEG_BIG)
    vmax = jnp.max(v, axis=1, keepdims=True)
    hit = jnp.logical_and(mask, v == vmax)
    idx = jnp.min(jnp.where(hit, lane_f, float(LANES)), axis=1, keepdims=True)
    return vmax, idx


def _router_kernel(x_ref, w_ref, b_ref, ids_ref, wts_ref, rank_ref, cnt_ref, carry_ref, *, n_groups, n_exp):
    i = pl.program_id(0)
    tm = x_ref.shape[0]

    @pl.when(i == 0)
    def _():
        carry_ref[...] = jnp.zeros_like(carry_ref)

    xh, xl = _split_bf16(x_ref[...])
    wh, wl = _split_bf16(w_ref[...])
    lg = (jnp.dot(xh, wh, preferred_element_type=F32) + jnp.dot(xh, wl, preferred_element_type=F32)
          + jnp.dot(xl, wh, preferred_element_type=F32)) + b_ref[...]
    lane = lax.broadcasted_iota(I32, lg.shape, 1)
    lane_f = lane.astype(F32)

    gmask = lane < n_groups
    gmax, gidx = _first_argmax(lg, gmask, lane_f)
    gsum = jnp.sum(jnp.where(gmask, jnp.exp(jnp.where(gmask, lg, NEG_BIG) - gmax), 0.0), axis=1, keepdims=True)
    p_group = 1.0 / gsum

    e_lo = float(n_groups) + gidx * float(n_exp)
    emask = jnp.logical_and(lane_f >= e_lo, lane_f < e_lo + float(n_exp))
    t1, i1 = _first_argmax(lg, emask, lane_f)
    emask2 = jnp.logical_and(emask, lane_f != i1)
    t2, i2 = _first_argmax(lg, emask2, lane_f)
    e21 = jnp.exp(t2 - t1)
    p1 = p_group / (1.0 + e21)
    p2 = p_group * e21 / (1.0 + e21)
    id1 = i1 - float(n_groups)
    id2 = i2 - float(n_groups)

    oh = jnp.where(lane_f == id1, 1.0, 0.0) + jnp.where(lane_f == id2, 1.0, 0.0)
    r = lax.broadcasted_iota(I32, (tm, tm), 0)
    cc = lax.broadcasted_iota(I32, (tm, tm), 1)
    tril = jnp.where(r > cc, 1.0, 0.0).astype(BF16)
    before = jnp.dot(tril, oh.astype(BF16), preferred_element_type=F32) + carry_ref[...]
    rank1 = jnp.sum(jnp.where(lane_f == id1, before, 0.0), axis=1, keepdims=True)
    rank2 = jnp.sum(jnp.where(lane_f == id2, before, 0.0), axis=1, keepdims=True)
    new_carry = carry_ref[...] + jnp.sum(oh, axis=0, keepdims=True)
    carry_ref[...] = new_carry
    cnt_ref[...] = jnp.broadcast_to(new_carry, cnt_ref.shape).astype(I32)

    first = lane == 0
    second = lane == 1
    ids_ref[...] = jnp.where(first, id1, jnp.where(second, id2, 0.0)).astype(I32)
    wts_ref[...] = jnp.where(first, p1, jnp.where(second, p2, 0.0))
    rank_ref[...] = jnp.where(first, rank1, jnp.where(second, rank2, 0.0)).astype(I32)


def moe_router(x, w_pad, b_pad, n_groups, n_exp, *, tm_pref=512):
    s, dm = x.shape
    tm = _tile(s, tm_pref)
    return pl.pallas_call(
        functools.partial(_router_kernel, n_groups=n_groups, n_exp=n_exp),
        grid=(s // tm,),
        in_specs=[pl.BlockSpec((tm, dm), lambda i: (i, 0)),
                  pl.BlockSpec((dm, LANES), lambda i: (0, 0)),
                  pl.BlockSpec((1, LANES), lambda i: (0, 0))],
        out_specs=[pl.BlockSpec((tm, LANES), lambda i: (i, 0)),
                   pl.BlockSpec((tm, LANES), lambda i: (i, 0)),
                   pl.BlockSpec((tm, LANES), lambda i: (i, 0)),
                   pl.BlockSpec((SUBLANES, LANES), lambda i: (0, 0))],
        out_shape=[jax.ShapeDtypeStruct((s, LANES), I32), jax.ShapeDtypeStruct((s, LANES), F32),
                   jax.ShapeDtypeStruct((s, LANES), I32), jax.ShapeDtypeStruct((SUBLANES, LANES), I32)],
        scratch_shapes=[pltpu.VMEM((1, LANES), F32)],
        compiler_params=_params(("arbitrary",)),
        name="moe_router",
    )(x, w_pad, b_pad)


ROW_DMA_PRIORITIES = (0, 1)


def _start_row_gather(src_hbm, dst, sem, idx_ref, base):
    for r in range(dst.shape[0]):
        tok = idx_ref[base + r]
        pltpu.make_async_copy(src_hbm.at[pl.ds(tok, 1), :], dst.at[pl.ds(r, 1), :], sem).start(
            priority=ROW_DMA_PRIORITIES[r % 2])


def _wait_row_gather(dst, sem):
    pltpu.make_async_copy(dst, dst, sem).wait()


def _moe_dispatch_kernel(pos_ref, ends_ref, x_ref, xs_hbm, zbuf, zsem, sem):
    i = pl.program_id(0)
    tm = x_ref.shape[0]
    n_exp = ends_ref.shape[0]

    @pl.when(i == 0)
    def _():
        zbuf[...] = jnp.zeros_like(zbuf)

        def last_tile_copy(e):
            start = pl.multiple_of(ends_ref[e] - tm, tm)
            return pltpu.make_async_copy(zbuf, xs_hbm.at[pl.ds(start, tm), :], zsem)

        def non_empty(e):
            return ends_ref[e] > (ends_ref[e - 1] if e > 0 else 0)

        def tail_tile_copy(t):
            return pltpu.make_async_copy(zbuf, xs_hbm.at[pl.ds(pl.multiple_of(t * tm, tm), tm), :], zsem)

        n_used = ends_ref[n_exp - 1] // tm
        n_tiles = xs_hbm.shape[0] // tm
        for e in range(n_exp):
            pl.when(non_empty(e))(lambda e=e: last_tile_copy(e).start())
        lax.fori_loop(n_used, n_tiles, lambda t, c: (tail_tile_copy(t).start(), c)[1], 0)
        for e in range(n_exp):
            pl.when(non_empty(e))(lambda e=e: last_tile_copy(e).wait())
        lax.fori_loop(n_used, n_tiles, lambda t, c: (tail_tile_copy(t).wait(), c)[1], 0)

    for r in range(tm):
        for kk in range(MOE_TOP_K):
            slot = pos_ref[(i * tm + r) * MOE_TOP_K + kk]
            pltpu.make_async_copy(x_ref.at[pl.ds(r, 1), :], xs_hbm.at[pl.ds(slot, 1), :], sem).start(
                priority=ROW_DMA_PRIORITIES[kk])
    for kk in range(MOE_TOP_K):
        pltpu.make_async_copy(x_ref, x_ref, sem).wait()


def moe_dispatch(x, pos_flat, ends, n_tiles, tm):
    s, dm = x.shape
    grid_spec = pltpu.PrefetchScalarGridSpec(
        num_scalar_prefetch=2,
        grid=(s // tm,),
        in_specs=[pl.BlockSpec((tm, dm), lambda i, pos, ends: (i, 0))],
        out_specs=pl.BlockSpec(memory_space=pl.ANY),
        scratch_shapes=[pltpu.VMEM((tm, dm), F32),
                        pltpu.SemaphoreType.DMA(()),
                        pltpu.SemaphoreType.DMA(())])
    return pl.pallas_call(
        _moe_dispatch_kernel,
        grid_spec=grid_spec,
        out_shape=jax.ShapeDtypeStruct((n_tiles * tm, dm), F32),
        compiler_params=_params(("arbitrary",)),
        name="moe_dispatch",
    )(pos_flat, ends, x)


def _moe_ffn_kernel(te_ref, nu_ref, x_ref, wg_ref, wu_ref, wd_ref, o_ref, wgub, wdb):
    i = pl.program_id(0)
    n_used = nu_ref[0]
    fe = wg_ref.shape[1]

    @pl.when(i < n_used)
    def _():
        is_new = jnp.logical_or(i == 0, te_ref[i] != te_ref[jnp.maximum(i - 1, 0)])

        @pl.when(is_new)
        def _():
            wgub[:, :fe] = wg_ref[...].astype(BF16)
            wgub[:, fe:] = wu_ref[...].astype(BF16)
            wdb[...] = wd_ref[...].astype(BF16)

        x = x_ref[...].astype(BF16)
        gate_up = jnp.dot(x, wgub[...], preferred_element_type=F32)
        gate, up = gate_up[:, :fe], gate_up[:, fe:]
        hmid = (gate * _sigmoid(gate) * up).astype(BF16)
        o_ref[...] = jnp.dot(hmid, wdb[...], preferred_element_type=F32)

    @pl.when(i >= n_used)
    def _():
        o_ref[...] = jnp.zeros_like(o_ref)


def moe_expert_ffn(xs, w_gate, w_up, w_down, layer, tile_expert, n_used, tm):
    dm = xs.shape[1]
    n_exp_total = MOE_GROUPS * MOE_EXPERTS_PER_GROUP
    fe = w_gate.shape[-1]
    n_tiles = tile_expert.shape[0]
    wg = w_gate.reshape(-1, dm, fe)
    wu = w_up.reshape(-1, dm, fe)
    wd = w_down.reshape(-1, fe, dm)
    base = layer * n_exp_total
    grid_spec = pltpu.PrefetchScalarGridSpec(
        num_scalar_prefetch=2,
        grid=(n_tiles,),
        in_specs=[pl.BlockSpec((tm, dm), lambda i, te, nu: (jnp.minimum(i, nu[0] - 1), 0)),
                  pl.BlockSpec((None, dm, fe), lambda i, te, nu: (base + te[i], 0, 0)),
                  pl.BlockSpec((None, dm, fe), lambda i, te, nu: (base + te[i], 0, 0)),
                  pl.BlockSpec((None, fe, dm), lambda i, te, nu: (base + te[i], 0, 0))],
        out_specs=pl.BlockSpec((tm, dm), lambda i, te, nu: (i, 0)),
        scratch_shapes=[pltpu.VMEM((dm, 2 * fe), BF16),
                        pltpu.VMEM((fe, dm), BF16)])
    return pl.pallas_call(
        _moe_ffn_kernel,
        grid_spec=grid_spec,
        out_shape=jax.ShapeDtypeStruct((n_tiles * tm, dm), F32),
        compiler_params=_params(("arbitrary",)),
        name="moe_expert_ffn",
    )(tile_expert, n_used, xs, wg, wu, wd)


def _moe_combine_kernel(pos_ref, y_hbm, x_ref, wts_ref, g_ref, b_ref, o_ref, ob_ref, ybuf, sem, *, alpha):
    i = pl.program_id(0)
    n = pl.num_programs(0)
    tm = x_ref.shape[0]

    def gather(tile, sl):
        for kk in range(MOE_TOP_K):
            _start_row_gather(y_hbm, ybuf.at[sl, kk], sem.at[sl], pos_ref, (kk * n + tile) * tm)

    @pl.when(i == 0)
    def _():
        gather(0, 0)

    def tile_body(slot, prefetch):
        _wait_row_gather(ybuf.at[slot], sem.at[slot])
        if prefetch:
            gather(i + 1, 1 - slot)
        wts = wts_ref[...]
        mix = wts[:, 0:1] * ybuf[slot, 0] + wts[:, 1:2] * ybuf[slot, 1]
        y = alpha * x_ref[...] + mix
        out = _layernorm_rows(y, g_ref[...], b_ref[...])
        o_ref[...] = out
        ob_ref[...] = out.astype(BF16)

    for slot in range(2):
        pl.when(jnp.logical_and(i + 1 < n, i % 2 == slot))(functools.partial(tile_body, slot, True))
        pl.when(jnp.logical_and(i + 1 == n, i % 2 == slot))(functools.partial(tile_body, slot, False))


def moe_combine_layernorm(y_sorted, pos_km, x, wts, g, b, alpha, *, tm_pref=256):
    s, dm = x.shape
    tm = _tile(s, tm_pref)
    grid_spec = pltpu.PrefetchScalarGridSpec(
        num_scalar_prefetch=1,
        grid=(s // tm,),
        in_specs=[pl.BlockSpec(memory_space=pl.ANY),
                  pl.BlockSpec((tm, dm), lambda i, pos: (i, 0)),
                  pl.BlockSpec((tm, LANES), lambda i, pos: (i, 0)),
                  pl.BlockSpec((1, dm), lambda i, pos: (0, 0)),
                  pl.BlockSpec((1, dm), lambda i, pos: (0, 0))],
        out_specs=[pl.BlockSpec((tm, dm), lambda i, pos: (i, 0)),
                   pl.BlockSpec((tm, dm), lambda i, pos: (i, 0))],
        scratch_shapes=[pltpu.VMEM((2, MOE_TOP_K, tm, dm), F32),
                        pltpu.SemaphoreType.DMA((2,))])
    return pl.pallas_call(
        functools.partial(_moe_combine_kernel, alpha=alpha),
        grid_spec=grid_spec,
        out_shape=[jax.ShapeDtypeStruct((s, dm), F32), jax.ShapeDtypeStruct((s, dm), BF16)],
        compiler_params=_params(("arbitrary",)),
        name="moe_combine_ln",
    )(pos_km, y_sorted, x, wts, g.reshape(1, dm), b.reshape(1, dm))


def _pad_cols(w, width=LANES):
    return jnp.pad(w, ((0, 0), (0, width - w.shape[1])))


def _moe_layer(x, x_bf16, layer, w_rg, b_rg, w_re, b_re, w_gate, w_up, w_down, ln_g, ln_b, alpha, *, tm=256):
    del x_bf16
    s, dm = x.shape
    n_groups, _, n_exp = w_re.shape
    n_total = n_groups * n_exp
    w_r = _pad_cols(jnp.concatenate([w_rg, jnp.transpose(w_re, (1, 0, 2)).reshape(dm, n_total)], axis=1))
    b_r = _pad_cols(jnp.concatenate([b_rg, b_re.reshape(n_total)])[None, :])
    ids, wts, ranks, counts = moe_router(x, w_r, b_r, n_groups, n_exp)

    n_tiles = (s * MOE_TOP_K) // tm + n_total
    counts = counts[0, :n_total]
    padded = ((counts + tm - 1) // tm) * tm
    ends = jnp.cumsum(padded).astype(I32)
    starts = ends - padded
    ids2 = ids[:, :MOE_TOP_K]
    start_of_pick = jnp.sum(jnp.where(ids2[..., None] == jnp.arange(n_total, dtype=I32), starts, 0), axis=-1)
    pos = (start_of_pick + ranks[:, :MOE_TOP_K]).astype(I32)
    n_used = (ends[-1] // tm).astype(I32)
    tile_start = jnp.arange(n_tiles, dtype=I32) * tm
    tile_expert = jnp.minimum(jnp.searchsorted(ends, tile_start, side="right"), n_total - 1).astype(I32)
    last_expert = tile_expert[jnp.maximum(n_used - 1, 0)]
    tile_expert = jnp.where(jnp.arange(n_tiles) < n_used, tile_expert, last_expert)

    xs = moe_dispatch(x, pos.reshape(-1), ends, n_tiles, tm)
    y_sorted = moe_expert_ffn(xs, w_gate, w_up, w_down, layer, tile_expert, n_used.reshape(1), tm)
    pos_km = jnp.transpose(pos.reshape(s // tm, tm, MOE_TOP_K), (2, 0, 1)).reshape(-1).astype(I32)
    return moe_combine_layernorm(y_sorted, pos_km, x, wts, ln_g, ln_b, alpha, tm_pref=tm)


def kernel(x, dn_w_in, dn_conv_w, dn_A_log, dn_dt_bias, dn_norm_w, dn_w_out, fox_w_in, fox_b_f, fox_q_norm_w,
           fox_k_norm_w, fox_w_out, ln1_g, ln1_b, ln2_g, ln2_b, moe_w_rg, moe_b_rg, moe_w_re, moe_b_re,
           moe_w_gate, moe_w_up, moe_w_down):
    batch, s, dm = x.shape
    assert batch == 1
    depth = ln1_g.shape[0]
    alpha = (2 * depth) ** 0.25
    x2d = x.reshape(s, dm)
    xb = x2d.astype(BF16)
    d = HEAD_DIM
    hb = 4

    for i in range(depth):
        j = i // 2
        if i % 2 == 0:
            n_v_heads = dn_A_log.shape[1]
            qk_dim = (n_v_heads // 2) * d
            v_dim = n_v_heads * d
            conv_dim = 2 * qk_dim + v_dim
            w_in_t = jnp.transpose(dn_w_in[j])
            proj = matmul(xb, w_in_t, conv_dim + v_dim)
            hg = n_v_heads // hb
            w_b = w_in_t[conv_dim + v_dim:conv_dim + v_dim + n_v_heads].reshape(hg, hb, dm)
            w_a = w_in_t[conv_dim + v_dim + n_v_heads:].reshape(hg, hb, dm)
            w_ba = jnp.pad(jnp.concatenate([w_b, w_a], axis=1), ((0, 0), (0, LANES - 2 * hb), (0, 0)))
            ba_g = matmul(xb, w_ba.reshape(hg * LANES, dm), hg * LANES)
            zeros = jnp.zeros((hg, hb), F32)
            alog_g = jnp.pad(jnp.concatenate([zeros, dn_A_log[j].reshape(hg, hb)], axis=1),
                             ((0, 0), (0, LANES - 2 * hb))).reshape(hg, 1, LANES)
            dt_g = jnp.pad(jnp.concatenate([zeros, dn_dt_bias[j].reshape(hg, hb)], axis=1),
                           ((0, 0), (0, LANES - 2 * hb))).reshape(hg, 1, LANES)
            qkv = gdn_preprocess(proj, dn_conv_w[j], qk_dim)
            mixed = gdn_delta_rule(qkv, proj, ba_g, alog_g, dt_g, dn_norm_w[j], n_v_heads, hb=hb)
            w_out = dn_w_out[j]
        else:
            n_heads = fox_b_f.shape[1]
            dim = n_heads * d
            w_in_t = jnp.transpose(fox_w_in[j])
            proj = matmul(xb, w_in_t, 4 * dim)
            w_f = jnp.pad(w_in_t[4 * dim:], ((0, LANES - n_heads), (0, 0)))
            f_logits = matmul(xb, w_f, LANES, tn_pref=LANES)
            c_pad = fox_cumulative_gate(f_logits, _pad_cols(fox_b_f[j][None, :]))
            qa, ka, v_t = fox_preprocess(proj, c_pad, fox_q_norm_w[j], fox_k_norm_w[j], n_heads)
            mixed = fox_attention(qa, ka, v_t, proj, n_heads)
            w_out = fox_w_out[j]
        x2d, xb = matmul_residual_layernorm(mixed, w_out.astype(BF16), x2d, ln1_g[i], ln1_b[i], alpha)
        x2d, xb = _moe_layer(x2d, xb, i, moe_w_rg[i], moe_b_rg[i], moe_w_re[i], moe_b_re[i],
                             moe_w_gate, moe_w_up, moe_w_down, ln2_g[i], ln2_b[i], alpha)
    return x2d.reshape(batch, s, dm)
```

```python
import functools
import math

import jax
import jax.numpy as jnp
from jax import lax
from jax.experimental import pallas as pl
from jax.experimental.pallas import tpu as pltpu

F32 = jnp.float32
BF16 = jnp.bfloat16
I32 = jnp.int32

LANES = 128
SUBLANES = 8
V7X_VMEM_BYTES = 64 * 1024 * 1024
VMEM_LIMIT = V7X_VMEM_BYTES - 8 * 1024 * 1024

HEAD_DIM = 128
SOLVE_CHUNK = 128
DN_CONV_WIDTH = 4
MOE_GROUPS = 4
MOE_EXPERTS_PER_GROUP = 8
MOE_TOP_K = 2
NORM_EPS = 1e-6
NEG_BIG = -1e30
LOG2E = 1.4426950408889634


def _params(sem, vmem=VMEM_LIMIT):
    return pltpu.CompilerParams(dimension_semantics=sem, vmem_limit_bytes=vmem)


def _tile(dim, pref):
    t = min(dim, pref)
    assert dim % t == 0, (dim, pref)
    return t


def _softplus(x):
    return jnp.maximum(x, 0.0) + jnp.log(1.0 + jnp.exp(-jnp.abs(x)))


def _sigmoid(x):
    return 1.0 / (1.0 + jnp.exp(-x))


def _mm_kernel(a_ref, wt_ref, o_ref):
    o_ref[...] = lax.dot_general(a_ref[...], wt_ref[...].astype(BF16), (((1,), (1,)), ((), ())),
                                 preferred_element_type=F32)


def matmul(a, w_t, n_out, *, tm_pref=2048, tn_pref=512):
    m, k = a.shape
    assert a.dtype == BF16
    tm = _tile(m, tm_pref)
    tn = _tile(n_out, tn_pref)
    return pl.pallas_call(
        _mm_kernel,
        grid=(m // tm, n_out // tn),
        in_specs=[pl.BlockSpec((tm, k), lambda i, j: (i, 0)),
                  pl.BlockSpec((tn, k), lambda i, j: (j, 0))],
        out_specs=pl.BlockSpec((tm, tn), lambda i, j: (i, j)),
        out_shape=jax.ShapeDtypeStruct((m, n_out), F32),
        compiler_params=_params(("parallel", "parallel")),
        name="proj_matmul",
    )(a, w_t)


def _layernorm_rows(y, g, b):
    mu = jnp.mean(y, axis=1, keepdims=True)
    yc = y - mu
    var = jnp.mean(yc * yc, axis=1, keepdims=True)
    return yc * lax.rsqrt(var + NORM_EPS) * g + b


def _mm_res_ln_kernel(a_ref, w_ref, res_ref, g_ref, b_ref, o_ref, ob_ref, *, alpha):
    y = alpha * res_ref[...] + jnp.dot(a_ref[...], w_ref[...], preferred_element_type=F32)
    out = _layernorm_rows(y, g_ref[...], b_ref[...])
    o_ref[...] = out
    ob_ref[...] = out.astype(BF16)


def matmul_residual_layernorm(a, w_bf16, res, g, b, alpha, *, tm_pref=512):
    m, k = a.shape
    n = w_bf16.shape[1]
    tm = _tile(m, tm_pref)
    return pl.pallas_call(
        functools.partial(_mm_res_ln_kernel, alpha=alpha),
        grid=(m // tm,),
        in_specs=[pl.BlockSpec((tm, k), lambda i: (i, 0)),
                  pl.BlockSpec((k, n), lambda i: (0, 0), pipeline_mode=pl.Buffered(1)),
                  pl.BlockSpec((tm, n), lambda i: (i, 0)),
                  pl.BlockSpec((1, n), lambda i: (0, 0)),
                  pl.BlockSpec((1, n), lambda i: (0, 0))],
        out_specs=[pl.BlockSpec((tm, n), lambda i: (i, 0)),
                   pl.BlockSpec((tm, n), lambda i: (i, 0))],
        out_shape=[jax.ShapeDtypeStruct((m, n), F32), jax.ShapeDtypeStruct((m, n), BF16)],
        compiler_params=_params(("parallel",)),
        name="out_proj_res_ln",
    )(a, w_bf16, res, g.reshape(1, n), b.reshape(1, n))


def _gdn_pre_kernel(cur_ref, halo_ref, w_ref, o_ref, ext_ref, *, n_qk_blocks, q_scale):
    i = pl.program_id(0)
    j = pl.program_id(1)
    tt, cb = cur_ref.shape
    keep = (i > 0).astype(F32)
    ext_ref[0:SUBLANES, :] = halo_ref[...] * keep
    ext_ref[SUBLANES:SUBLANES + tt, :] = cur_ref[...]
    w = w_ref[...]
    acc = cur_ref[...] * w[DN_CONV_WIDTH - 1:DN_CONV_WIDTH]
    for kk in range(DN_CONV_WIDTH - 1):
        off = SUBLANES - (DN_CONV_WIDTH - 1) + kk
        acc = acc + ext_ref[off:off + tt, :] * w[kk:kk + 1]
    y = acc * _sigmoid(acc)

    def write_normed(scale):
        for h in range(cb // HEAD_DIM):
            yh = y[:, h * HEAD_DIM:(h + 1) * HEAD_DIM]
            r = lax.rsqrt(jnp.sum(yh * yh, axis=1, keepdims=True) + NORM_EPS)
            o_ref[:, h * HEAD_DIM:(h + 1) * HEAD_DIM] = yh * (r * scale if scale != 1.0 else r)

    @pl.when(j < n_qk_blocks)
    def _():
        write_normed(q_scale)

    @pl.when(jnp.logical_and(j >= n_qk_blocks, j < 2 * n_qk_blocks))
    def _():
        write_normed(1.0)

    @pl.when(j >= 2 * n_qk_blocks)
    def _():
        o_ref[...] = y


def gdn_preprocess(proj, conv_w, qk_dim, *, tt_pref=1024, cb_pref=1024):
    s = proj.shape[0]
    conv_dim = conv_w.shape[1]
    tt = _tile(s, tt_pref)
    cb = _tile(qk_dim, cb_pref)
    assert conv_dim % cb == 0 and tt % SUBLANES == 0
    hb = tt // SUBLANES
    return pl.pallas_call(
        functools.partial(_gdn_pre_kernel, n_qk_blocks=qk_dim // cb, q_scale=HEAD_DIM ** -0.5),
        grid=(s // tt, conv_dim // cb),
        in_specs=[pl.BlockSpec((tt, cb), lambda i, j: (i, j)),
                  pl.BlockSpec((SUBLANES, cb), lambda i, j: (jnp.maximum(i * hb - 1, 0), j)),
                  pl.BlockSpec((DN_CONV_WIDTH, cb), lambda i, j: (0, j))],
        out_specs=pl.BlockSpec((tt, cb), lambda i, j: (i, j)),
        out_shape=jax.ShapeDtypeStruct((s, conv_dim), F32),
        scratch_shapes=[pltpu.VMEM((tt + SUBLANES, cb), F32)],
        compiler_params=_params(("parallel", "parallel")),
        name="gdn_conv_norm",
    )(proj, proj, conv_w)


def _bmm(a, b):
    return jnp.einsum("nij,njk->nik", a.astype(BF16), b.astype(BF16), preferred_element_type=F32)


def _bmm_nt(a, b):
    return jnp.einsum("nid,njd->nij", a.astype(BF16), b.astype(BF16), preferred_element_type=F32)


def _chunk_cumsum(x, chunk, reverse=False):
    rows = x.shape[0]
    pos = lax.broadcasted_iota(I32, x.shape, 0) % chunk
    step = 1
    while step < chunk:
        if reverse:
            shifted = pltpu.roll(x, rows - step, axis=0)
            x = x + jnp.where(pos + step < chunk, shifted, 0.0)
        else:
            shifted = pltpu.roll(x, step, axis=0)
            x = x + jnp.where(pos >= step, shifted, 0.0)
        step *= 2
    return x


def _cat_lanes(a, b):
    return jnp.concatenate([a, b], axis=-1)


def _block_diag(a, b):
    return jnp.concatenate([_cat_lanes(a, jnp.zeros_like(b)), _cat_lanes(jnp.zeros_like(a), b)], axis=-2)


def _unit_lower_inverse_pair(lower_a, lower_b, eye):
    c = lower_a.shape[-1]
    pa, pb = lower_a.astype(BF16), lower_b.astype(BF16)
    tinv = _cat_lanes(eye - lower_a, eye - lower_b)
    span = 2
    while span < c:
        power = jnp.einsum("nij,njk->nik", _cat_lanes(pa, pb), _block_diag(pa, pb), preferred_element_type=F32)
        pa, pb = power[..., :c].astype(BF16), power[..., c:].astype(BF16)
        tinv = tinv + jnp.einsum("nij,njk->nik", tinv.astype(BF16), _block_diag(pa, pb),
                                 preferred_element_type=F32)
        span *= 2
    return tinv[..., :c], tinv[..., c:]


def _delta_kernel(q_ref, k_ref, v_ref, z_ref, ba_ref, alog_ref, dt_ref, nw_ref, o_ref,
                  state_ref, u_ref, wq_ref, ak_ref, egl_ref, osc_ref, *, hb):
    t = pl.program_id(1)
    tb = q_ref.shape[0]
    c = SOLVE_CHUNK
    nc = tb // c
    d = HEAD_DIM

    @pl.when(t == 0)
    def _():
        state_ref[...] = jnp.zeros_like(state_ref)

    ba = ba_ref[...]
    beta_all = _sigmoid(ba)
    g_all = -jnp.exp(alog_ref[...]) * _softplus(ba + dt_ref[...])
    big_g = _chunk_cumsum(g_all, c)
    tail_g = _chunk_cumsum(g_all, c, reverse=True) - g_all
    e_g = jnp.exp(big_g)
    e_tail = jnp.exp(tail_g)
    e_tot = jnp.exp(big_g + tail_g)
    big_g_t = big_g.T

    row = lax.broadcasted_iota(I32, (c, c), 0)
    col = lax.broadcasted_iota(I32, (c, c), 1)
    causal = row >= col
    strict = row > col
    eye = (row == col).astype(F32)[None]

    def head_terms(hh):
        gi = hb + hh
        gcol_b = jnp.broadcast_to(big_g[:, gi:gi + 1], (tb, c))
        decays = []
        for n in range(nc):
            diff = gcol_b[n * c:(n + 1) * c, :] - big_g_t[gi:gi + 1, n * c:(n + 1) * c]
            decays.append(jnp.where(causal, jnp.exp(jnp.where(causal, diff, 0.0)), 0.0))
        return beta_all[:, hh:hh + 1], e_g[:, gi:gi + 1], e_tail[:, gi:gi + 1], jnp.stack(decays, axis=0)

    for p in range(hb // 2):
        heads = (2 * p, 2 * p + 1)
        qh = q_ref[:, p * d:(p + 1) * d]
        kh = k_ref[:, p * d:(p + 1) * d]
        terms = [head_terms(hh) for hh in heads]
        kbs = [kh * tm_[0] for tm_ in terms]
        stacked = jnp.concatenate([kbs[0].reshape(nc, c, d), kbs[1].reshape(nc, c, d), qh.reshape(nc, c, d)],
                                  axis=1)
        prod = _bmm_nt(stacked, kh.reshape(nc, c, d))
        qk = prod[:, 2 * c:]
        lowers = [jnp.where(strict[None], prod[:, i * c:(i + 1) * c] * terms[i][3], 0.0) for i in range(2)]
        tinvs = _unit_lower_inverse_pair(lowers[0], lowers[1], eye)
        packed = []
        for i, hh in enumerate(heads):
            bcol, egcol, etcol, decay = terms[i]
            vh = v_ref[:, hh * d:(hh + 1) * d]
            rhs = _cat_lanes((vh * bcol).reshape(nc, c, d), (kbs[i] * egcol).reshape(nc, c, d))
            sol = _bmm(tinvs[i], rhs)
            u_ref[hh] = sol[..., :d]
            wq_i = jnp.concatenate([sol[..., d:], (qh * egcol).reshape(nc, c, d)], axis=1).astype(BF16)
            kd_t = (kh * etcol).T.astype(BF16)
            kdt3 = jnp.stack([kd_t[:, n * c:(n + 1) * c] for n in range(nc)], axis=0)
            packed.append((wq_i, jnp.concatenate([(qk * decay).astype(BF16), kdt3], axis=1)))
            egl_ref[hh] = jnp.broadcast_to(e_tot[:, hb + hh:hb + hh + 1], (tb, d))
        wq_ref[p] = _cat_lanes(packed[0][0], packed[1][0])
        ak_ref[p] = _cat_lanes(packed[0][1], packed[1][1])

    states = [state_ref[hh] for hh in range(hb)]
    for n in range(nc):
        r0 = n * c
        for p in range(hb // 2):
            ha, hb_ = 2 * p, 2 * p + 1
            sa, sb = states[ha], states[hb_]
            ws = jnp.dot(wq_ref[p, n], _block_diag(sa.astype(BF16), sb.astype(BF16)),
                         preferred_element_type=F32)
            vna = u_ref[ha, n] - ws[:c, :d]
            vnb = u_ref[hb_, n] - ws[:c, d:]
            r = jnp.dot(ak_ref[p, n], _block_diag(vna.astype(BF16), vnb.astype(BF16)),
                        preferred_element_type=F32)
            osc_ref[r0:r0 + c, ha * d:(ha + 1) * d] = ws[c:, :d] + r[:c, :d]
            osc_ref[r0:r0 + c, hb_ * d:(hb_ + 1) * d] = ws[c:, d:] + r[:c, d:]
            states[ha] = sa * egl_ref[ha, r0:r0 + 1, :] + r[c:, :d]
            states[hb_] = sb * egl_ref[hb_, r0:r0 + 1, :] + r[c:, d:]
    for hh in range(hb):
        state_ref[hh] = states[hh]

    nw = nw_ref[...]
    for hh in range(hb):
        o = osc_ref[:, hh * d:(hh + 1) * d]
        z = z_ref[:, hh * d:(hh + 1) * d]
        ms = jnp.mean(o * o, axis=1, keepdims=True)
        y = o * lax.rsqrt(ms + NORM_EPS) * nw * (z * _sigmoid(z))
        o_ref[:, hh * d:(hh + 1) * d] = y.astype(BF16)


def gdn_delta_rule(qkv, proj, ba_g, alog_g, dt_g, norm_w, n_v_heads, *, hb, tb_pref=1024):
    s = qkv.shape[0]
    d = HEAD_DIM
    c = SOLVE_CHUNK
    qk_dim = (n_v_heads // 2) * d
    v_dim = n_v_heads * d
    tb = _tile(s, tb_pref)
    nc = tb // c
    hg = n_v_heads // hb
    qw = (hb // 2) * d
    vw = hb * d
    return pl.pallas_call(
        functools.partial(_delta_kernel, hb=hb),
        grid=(hg, s // tb),
        in_specs=[pl.BlockSpec((tb, qw), lambda g, t: (t, g)),
                  pl.BlockSpec((tb, qw), lambda g, t: (t, qk_dim // qw + g)),
                  pl.BlockSpec((tb, vw), lambda g, t: (t, 2 * qk_dim // vw + g)),
                  pl.BlockSpec((tb, vw), lambda g, t: (t, (2 * qk_dim + v_dim) // vw + g)),
                  pl.BlockSpec((tb, LANES), lambda g, t: (t, g)),
                  pl.BlockSpec((None, 1, LANES), lambda g, t: (g, 0, 0)),
                  pl.BlockSpec((None, 1, LANES), lambda g, t: (g, 0, 0)),
                  pl.BlockSpec((1, d), lambda g, t: (0, 0))],
        out_specs=pl.BlockSpec((tb, vw), lambda g, t: (t, g)),
        out_shape=jax.ShapeDtypeStruct((s, v_dim), BF16),
        scratch_shapes=[pltpu.VMEM((hb, d, d), F32),
                        pltpu.VMEM((hb, nc, c, d), F32),
                        pltpu.VMEM((hb // 2, nc, 2 * c, 2 * d), BF16),
                        pltpu.VMEM((hb // 2, nc, c + d, 2 * c), BF16),
                        pltpu.VMEM((hb, tb, d), F32),
                        pltpu.VMEM((tb, vw), F32)],
        compiler_params=_params(("parallel", "arbitrary")),
        name="gdn_delta_rule",
    )(qkv, qkv, qkv, proj, ba_g, alog_g, dt_g, norm_w.reshape(1, d))


def _split3_bf16(x):
    hi = x.astype(BF16).astype(F32)
    r1 = x - hi
    mid = r1.astype(BF16).astype(F32)
    return hi, mid, r1 - mid


def _fox_pre_kernel(q_ref, k_ref, v_ref, c_ref, qw_ref, kw_ref, qa_ref, ka_ref, vt_ref, *, n_heads):
    d = HEAD_DIM
    tt = q_ref.shape[0]
    c2 = c_ref[...] * LOG2E
    lane = lax.broadcasted_iota(I32, (tt, d), 1)
    qw = qw_ref[...] * (d ** -0.5 * LOG2E)
    kw = kw_ref[...]
    for h in range(n_heads):
        qh = q_ref[:, h * d:(h + 1) * d]
        kh = k_ref[:, h * d:(h + 1) * d]
        qn = qh * lax.rsqrt(jnp.mean(qh * qh, axis=1, keepdims=True) + NORM_EPS) * qw
        kn = kh * lax.rsqrt(jnp.mean(kh * kh, axis=1, keepdims=True) + NORM_EPS) * kw
        hi, mid, lo = _split3_bf16(c2[:, h:h + 1])
        aq = jnp.where(lane == 0, hi, jnp.where(lane == 1, mid, jnp.where(lane == 2, lo,
                                                                          jnp.where(lane < 6, 1.0, 0.0))))
        ak = jnp.where(lane < 3, 1.0, jnp.where(lane == 3, -hi, jnp.where(lane == 4, -mid,
                                                                           jnp.where(lane == 5, -lo, 0.0))))
        qa_ref[:, 2 * h * d:(2 * h + 1) * d] = qn.astype(BF16)
        qa_ref[:, (2 * h + 1) * d:(2 * h + 2) * d] = aq.astype(BF16)
        ka_ref[:, 2 * h * d:(2 * h + 1) * d] = kn.astype(BF16)
        ka_ref[:, (2 * h + 1) * d:(2 * h + 2) * d] = ak.astype(BF16)
    vt_ref[...] = v_ref[...].T.astype(BF16)


def fox_preprocess(proj, c_pad, q_norm_w, k_norm_w, n_heads, *, tt_pref=256):
    s = proj.shape[0]
    d = HEAD_DIM
    dim = n_heads * d
    tt = _tile(s, tt_pref)
    return pl.pallas_call(
        functools.partial(_fox_pre_kernel, n_heads=n_heads),
        grid=(s // tt,),
        in_specs=[pl.BlockSpec((tt, dim), lambda i: (i, 0)),
                  pl.BlockSpec((tt, dim), lambda i: (i, 1)),
                  pl.BlockSpec((tt, dim), lambda i: (i, 2)),
                  pl.BlockSpec((tt, LANES), lambda i: (i, 0)),
                  pl.BlockSpec((1, d), lambda i: (0, 0)),
                  pl.BlockSpec((1, d), lambda i: (0, 0))],
        out_specs=[pl.BlockSpec((tt, 2 * dim), lambda i: (i, 0)),
                   pl.BlockSpec((tt, 2 * dim), lambda i: (i, 0)),
                   pl.BlockSpec((dim, tt), lambda i: (0, i))],
        out_shape=[jax.ShapeDtypeStruct((s, 2 * dim), BF16), jax.ShapeDtypeStruct((s, 2 * dim), BF16),
                   jax.ShapeDtypeStruct((dim, s), BF16)],
        compiler_params=_params(("parallel",)),
        name="fox_qkv_prep",
    )(proj, proj, proj, c_pad, q_norm_w.reshape(1, d), k_norm_w.reshape(1, d))


def _fox_cumgate_kernel(f_ref, b_ref, o_ref, carry_ref):
    i = pl.program_id(0)
    tt = f_ref.shape[0]

    @pl.when(i == 0)
    def _():
        carry_ref[...] = jnp.zeros_like(carry_ref)

    x = f_ref[...] + b_ref[...]
    log_f = jnp.minimum(x, 0.0) - jnp.log(1.0 + jnp.exp(-jnp.abs(x)))
    csum = _chunk_cumsum(log_f, tt) + carry_ref[...]
    o_ref[...] = csum
    carry_ref[...] = csum[tt - 1:tt, :]


def fox_cumulative_gate(f_logits, b_row, *, tt_pref=512):
    s = f_logits.shape[0]
    tt = _tile(s, tt_pref)
    return pl.pallas_call(
        _fox_cumgate_kernel,
        grid=(s // tt,),
        in_specs=[pl.BlockSpec((tt, LANES), lambda i: (i, 0)),
                  pl.BlockSpec((1, LANES), lambda i: (0, 0))],
        out_specs=pl.BlockSpec((tt, LANES), lambda i: (i, 0)),
        out_shape=jax.ShapeDtypeStruct((s, LANES), F32),
        scratch_shapes=[pltpu.VMEM((1, LANES), F32)],
        compiler_params=_params(("arbitrary",)),
        name="fox_cumulative_gate",
    )(f_logits, b_row)


def _fox_attn_kernel(q_ref, k_ref, vt_ref, gate_ref, o_ref, m_ref, l_ref, acc_ref):
    qi = pl.program_id(1)
    tq = q_ref.shape[0]
    tk = tq
    q = q_ref[...]

    m_ref[...] = jnp.full_like(m_ref, NEG_BIG)
    l_ref[...] = jnp.zeros_like(l_ref)
    acc_ref[...] = jnp.zeros_like(acc_ref)

    def block(kj, masked, state):
        m_old, l_old, acc_old = state
        k0 = pl.multiple_of(kj * tk, tk)
        kb = k_ref[pl.ds(k0, tk), :]
        st = lax.dot_general(kb, q, (((1,), (1,)), ((), ())), preferred_element_type=F32)
        if masked:
            kpos = lax.broadcasted_iota(I32, (tk, tq), 0)
            qpos = lax.broadcasted_iota(I32, (tk, tq), 1)
            st = jnp.where(kpos <= qpos, st, -jnp.inf)
        m_new = jnp.maximum(m_old, jnp.max(st, axis=0, keepdims=True))
        alpha = jnp.exp2(m_old - m_new)
        p = jnp.exp2(st - m_new)
        l_new = alpha * l_old + jnp.sum(p, axis=0, keepdims=True)
        vb = vt_ref[:, pl.ds(k0, tk)]
        acc_new = alpha * acc_old + jnp.dot(vb, p.astype(BF16), preferred_element_type=F32)
        return m_new, l_new, acc_new

    def load_state():
        return m_ref[...], l_ref[...], acc_ref[...]

    def store_state(state):
        m_ref[...], l_ref[...], acc_ref[...] = state

    def full_body(kj, carry):
        store_state(block(kj, False, load_state()))
        return carry

    lax.fori_loop(0, qi, full_body, 0)
    _, l_fin, acc_fin = block(qi, True, load_state())

    o_t = acc_fin / l_fin
    o_ref[...] = (o_t.T * _sigmoid(gate_ref[...])).astype(BF16)


def fox_attention(qa, ka, v_t, proj, n_heads, *, tq_pref=1024):
    s = qa.shape[0]
    d = HEAD_DIM
    tq = _tile(s, tq_pref)
    return pl.pallas_call(
        _fox_attn_kernel,
        grid=(n_heads, s // tq),
        in_specs=[pl.BlockSpec((tq, 2 * d), lambda h, i: (i, h)),
                  pl.BlockSpec((s, 2 * d), lambda h, i: (0, h)),
                  pl.BlockSpec((d, s), lambda h, i: (h, 0)),
                  pl.BlockSpec((tq, d), lambda h, i: (i, 3 * n_heads + h))],
        out_specs=pl.BlockSpec((tq, d), lambda h, i: (i, h)),
        out_shape=jax.ShapeDtypeStruct((s, n_heads * d), BF16),
        scratch_shapes=[pltpu.VMEM((1, tq), F32), pltpu.VMEM((1, tq), F32), pltpu.VMEM((d, tq), F32)],
        compiler_params=_params(("parallel", "arbitrary")),
        name="fox_attention",
    )(qa, ka, v_t, proj)


def _split_bf16(x):
    hi = x.astype(BF16)
    lo = (x - hi.astype(F32)).astype(BF16)
    return hi, lo


def _first_argmax(vals, mask, lane_f):
    v = jnp.where(mask, vals, NEG_BIG)
    vmax = jnp.max(v, axis=1, keepdims=True)
    hit = jnp.logical_and(mask, v == vmax)
    idx = jnp.min(jnp.where(hit, lane_f, float(LANES)), axis=1, keepdims=True)
    return vmax, idx


def _router_kernel(x_ref, w_ref, b_ref, ids_ref, wts_ref, rank_ref, cnt_ref, carry_ref, *, n_groups, n_exp):
    i = pl.program_id(0)
    tm = x_ref.shape[0]

    @pl.when(i == 0)
    def _():
        carry_ref[...] = jnp.zeros_like(carry_ref)

    xh, xl = _split_bf16(x_ref[...])
    wh, wl = _split_bf16(w_ref[...])
    lg = (jnp.dot(xh, wh, preferred_element_type=F32) + jnp.dot(xh, wl, preferred_element_type=F32)
          + jnp.dot(xl, wh, preferred_element_type=F32)) + b_ref[...]
    lane = lax.broadcasted_iota(I32, lg.shape, 1)
    lane_f = lane.astype(F32)

    gmask = lane < n_groups
    gmax, gidx = _first_argmax(lg, gmask, lane_f)
    gsum = jnp.sum(jnp.where(gmask, jnp.exp(jnp.where(gmask, lg, NEG_BIG) - gmax), 0.0), axis=1, keepdims=True)
    p_group = 1.0 / gsum

    e_lo = float(n_groups) + gidx * float(n_exp)
    emask = jnp.logical_and(lane_f >= e_lo, lane_f < e_lo + float(n_exp))
    t1, i1 = _first_argmax(lg, emask, lane_f)
    emask2 = jnp.logical_and(emask, lane_f != i1)
    t2, i2 = _first_argmax(lg, emask2, lane_f)
    e21 = jnp.exp(t2 - t1)
    p1 = p_group / (1.0 + e21)
    p2 = p_group * e21 / (1.0 + e21)
    id1 = i1 - float(n_groups)
    id2 = i2 - float(n_groups)

    oh = jnp.where(lane_f == id1, 1.0, 0.0) + jnp.where(lane_f == id2, 1.0, 0.0)
    r = lax.broadcasted_iota(I32, (tm, tm), 0)
    cc = lax.broadcasted_iota(I32, (tm, tm), 1)
    tril = jnp.where(r > cc, 1.0, 0.0).astype(BF16)
    before = jnp.dot(tril, oh.astype(BF16), preferred_element_type=F32) + carry_ref[...]
    rank1 = jnp.sum(jnp.where(lane_f == id1, before, 0.0), axis=1, keepdims=True)
    rank2 = jnp.sum(jnp.where(lane_f == id2, before, 0.0), axis=1, keepdims=True)
    new_carry = carry_ref[...] + jnp.sum(oh, axis=0, keepdims=True)
    carry_ref[...] = new_carry
    cnt_ref[...] = jnp.broadcast_to(new_carry, cnt_ref.shape).astype(I32)

    first = lane == 0
    second = lane == 1
    ids_ref[...] = jnp.where(first, id1, jnp.where(second, id2, 0.0)).astype(I32)
    wts_ref[...] = jnp.where(first, p1, jnp.where(second, p2, 0.0))
    rank_ref[...] = jnp.where(first, rank1, jnp.where(second, rank2, 0.0)).astype(I32)


def moe_router(x, w_pad, b_pad, n_groups, n_exp, *, tm_pref=512):
    s, dm = x.shape
    tm = _tile(s, tm_pref)
    return pl.pallas_call(
        functools.partial(_router_kernel, n_groups=n_groups, n_exp=n_exp),
        grid=(s // tm,),
        in_specs=[pl.BlockSpec((tm, dm), lambda i: (i, 0)),
                  pl.BlockSpec((dm, LANES), lambda i: (0, 0)),
                  pl.BlockSpec((1, LANES), lambda i: (0, 0))],
        out_specs=[pl.BlockSpec((tm, LANES), lambda i: (i, 0)),
                   pl.BlockSpec((tm, LANES), lambda i: (i, 0)),
                   pl.BlockSpec((tm, LANES), lambda i: (i, 0)),
                   pl.BlockSpec((SUBLANES, LANES), lambda i: (0, 0))],
        out_shape=[jax.ShapeDtypeStruct((s, LANES), I32), jax.ShapeDtypeStruct((s, LANES), F32),
                   jax.ShapeDtypeStruct((s, LANES), I32), jax.ShapeDtypeStruct((SUBLANES, LANES), I32)],
        scratch_shapes=[pltpu.VMEM((1, LANES), F32)],
        compiler_params=_params(("arbitrary",)),
        name="moe_router",
    )(x, w_pad, b_pad)


ROW_DMA_PRIORITIES = (0, 1)


def _start_row_gather(src_hbm, dst, sem, idx_ref, base):
    for r in range(dst.shape[0]):
        tok = idx_ref[base + r]
        pltpu.make_async_copy(src_hbm.at[pl.ds(tok, 1), :], dst.at[pl.ds(r, 1), :], sem).start(
            priority=ROW_DMA_PRIORITIES[r % 2])


def _wait_row_gather(dst, sem):
    pltpu.make_async_copy(dst, dst, sem).wait()


def _moe_dispatch_kernel(pos_ref, ends_ref, x_ref, xs_hbm, zbuf, zsem, sem):
    i = pl.program_id(0)
    tm = x_ref.shape[0]
    n_exp = ends_ref.shape[0]

    @pl.when(i == 0)
    def _():
        zbuf[...] = jnp.zeros_like(zbuf)

        def last_tile_copy(e):
            start = pl.multiple_of(ends_ref[e] - tm, tm)
            return pltpu.make_async_copy(zbuf, xs_hbm.at[pl.ds(start, tm), :], zsem)

        def non_empty(e):
            return ends_ref[e] > (ends_ref[e - 1] if e > 0 else 0)

        def tail_tile_copy(t):
            return pltpu.make_async_copy(zbuf, xs_hbm.at[pl.ds(pl.multiple_of(t * tm, tm), tm), :], zsem)

        n_used = ends_ref[n_exp - 1] // tm
        n_tiles = xs_hbm.shape[0] // tm
        for e in range(n_exp):
            pl.when(non_empty(e))(lambda e=e: last_tile_copy(e).start())
        lax.fori_loop(n_used, n_tiles, lambda t, c: (tail_tile_copy(t).start(), c)[1], 0)
        for e in range(n_exp):
            pl.when(non_empty(e))(lambda e=e: last_tile_copy(e).wait())
        lax.fori_loop(n_used, n_tiles, lambda t, c: (tail_tile_copy(t).wait(), c)[1], 0)

    for r in range(tm):
        for kk in range(MOE_TOP_K):
            slot = pos_ref[(i * tm + r) * MOE_TOP_K + kk]
            pltpu.make_async_copy(x_ref.at[pl.ds(r, 1), :], xs_hbm.at[pl.ds(slot, 1), :], sem).start(
                priority=ROW_DMA_PRIORITIES[kk])
    for kk in range(MOE_TOP_K):
        pltpu.make_async_copy(x_ref, x_ref, sem).wait()


def moe_dispatch(x, pos_flat, ends, n_tiles, tm):
    s, dm = x.shape
    grid_spec = pltpu.PrefetchScalarGridSpec(
        num_scalar_prefetch=2,
        grid=(s // tm,),
        in_specs=[pl.BlockSpec((tm, dm), lambda i, pos, ends: (i, 0))],
        out_specs=pl.BlockSpec(memory_space=pl.ANY),
        scratch_shapes=[pltpu.VMEM((tm, dm), F32),
                        pltpu.SemaphoreType.DMA(()),
                        pltpu.SemaphoreType.DMA(())])
    return pl.pallas_call(
        _moe_dispatch_kernel,
        grid_spec=grid_spec,
        out_shape=jax.ShapeDtypeStruct((n_tiles * tm, dm), F32),
        compiler_params=_params(("arbitrary",)),
        name="moe_dispatch",
    )(pos_flat, ends, x)


def _moe_ffn_kernel(te_ref, nu_ref, nxt_ref, par_ref, x_ref, wg_hbm, wu_hbm, wd_hbm, o_ref,
                    wg_buf, wu_buf, wd_buf, wsem, wgub, wdb, *, base):
    i = pl.program_id(0)
    n_used = nu_ref[0]
    fe = wg_buf.shape[2]

    def weight_copies(expert, slot):
        e = base + expert
        return (pltpu.make_async_copy(wg_hbm.at[e], wg_buf.at[slot], wsem.at[slot]),
                pltpu.make_async_copy(wu_hbm.at[e], wu_buf.at[slot], wsem.at[slot]),
                pltpu.make_async_copy(wd_hbm.at[e], wd_buf.at[slot], wsem.at[slot]))

    @pl.when(i < n_used)
    def _():
        is_new = jnp.logical_or(i == 0, te_ref[i] != te_ref[jnp.maximum(i - 1, 0)])

        @pl.when(is_new)
        def _():
            slot = par_ref[i]

            @pl.when(i == 0)
            def _():
                for cp in weight_copies(te_ref[0], 0):
                    cp.start()

            for cp in weight_copies(te_ref[i], slot):
                cp.wait()

            @pl.when(nxt_ref[i] >= 0)
            def _():
                for cp in weight_copies(nxt_ref[i], 1 - slot):
                    cp.start()

            wgub[:, :fe] = wg_buf[slot].astype(BF16)
            wgub[:, fe:] = wu_buf[slot].astype(BF16)
            wdb[...] = wd_buf[slot].astype(BF16)

        x = x_ref[...].astype(BF16)
        gate_up = jnp.dot(x, wgub[...], preferred_element_type=F32)
        gate, up = gate_up[:, :fe], gate_up[:, fe:]
        hmid = (gate * _sigmoid(gate) * up).astype(BF16)
        o_ref[...] = jnp.dot(hmid, wdb[...], preferred_element_type=F32)

    @pl.when(i >= n_used)
    def _():
        o_ref[...] = jnp.zeros_like(o_ref)


def moe_expert_ffn(xs, w_gate, w_up, w_down, layer, tile_expert, n_used, next_expert, expert_parity, tm):
    dm = xs.shape[1]
    n_exp_total = MOE_GROUPS * MOE_EXPERTS_PER_GROUP
    fe = w_gate.shape[-1]
    n_tiles = tile_expert.shape[0]
    wg = w_gate.reshape(-1, dm, fe)
    wu = w_up.reshape(-1, dm, fe)
    wd = w_down.reshape(-1, fe, dm)
    grid_spec = pltpu.PrefetchScalarGridSpec(
        num_scalar_prefetch=4,
        grid=(n_tiles,),
        in_specs=[pl.BlockSpec((tm, dm), lambda i, te, nu, nxt, par: (jnp.minimum(i, nu[0] - 1), 0)),
                  pl.BlockSpec(memory_space=pl.ANY),
                  pl.BlockSpec(memory_space=pl.ANY),
                  pl.BlockSpec(memory_space=pl.ANY)],
        out_specs=pl.BlockSpec((tm, dm), lambda i, te, nu, nxt, par: (i, 0)),
        scratch_shapes=[pltpu.VMEM((2, dm, fe), F32),
                        pltpu.VMEM((2, dm, fe), F32),
                        pltpu.VMEM((2, fe, dm), F32),
                        pltpu.SemaphoreType.DMA((2,)),
                        pltpu.VMEM((dm, 2 * fe), BF16),
                        pltpu.VMEM((fe, dm), BF16)])
    return pl.pallas_call(
        functools.partial(_moe_ffn_kernel, base=layer * n_exp_total),
        grid_spec=grid_spec,
        out_shape=jax.ShapeDtypeStruct((n_tiles * tm, dm), F32),
        compiler_params=_params(("arbitrary",)),
        name="moe_expert_ffn",
    )(tile_expert, n_used, next_expert, expert_parity, xs, wg, wu, wd)


def _moe_combine_kernel(pos_ref, y_hbm, x_ref, wts_ref, g_ref, b_ref, o_ref, ob_ref, ybuf, sem, *, alpha):
    i = pl.program_id(0)
    n = pl.num_programs(0)
    tm = x_ref.shape[0]

    def gather(tile, sl):
        for kk in range(MOE_TOP_K):
            _start_row_gather(y_hbm, ybuf.at[sl, kk], sem.at[sl], pos_ref, (kk * n + tile) * tm)

    @pl.when(i == 0)
    def _():
        gather(0, 0)

    def tile_body(slot, prefetch):
        _wait_row_gather(ybuf.at[slot], sem.at[slot])
        if prefetch:
            gather(i + 1, 1 - slot)
        wts = wts_ref[...]
        mix = wts[:, 0:1] * ybuf[slot, 0] + wts[:, 1:2] * ybuf[slot, 1]
        y = alpha * x_ref[...] + mix
        out = _layernorm_rows(y, g_ref[...], b_ref[...])
        o_ref[...] = out
        ob_ref[...] = out.astype(BF16)

    for slot in range(2):
        pl.when(jnp.logical_and(i + 1 < n, i % 2 == slot))(functools.partial(tile_body, slot, True))
        pl.when(jnp.logical_and(i + 1 == n, i % 2 == slot))(functools.partial(tile_body, slot, False))


def moe_combine_layernorm(y_sorted, pos_km, x, wts, g, b, alpha, *, tm_pref=256):
    s, dm = x.shape
    tm = _tile(s, tm_pref)
    grid_spec = pltpu.PrefetchScalarGridSpec(
        num_scalar_prefetch=1,
        grid=(s // tm,),
        in_specs=[pl.BlockSpec(memory_space=pl.ANY),
                  pl.BlockSpec((tm, dm), lambda i, pos: (i, 0)),
                  pl.BlockSpec((tm, LANES), lambda i, pos: (i, 0)),
                  pl.BlockSpec((1, dm), lambda i, pos: (0, 0)),
                  pl.BlockSpec((1, dm), lambda i, pos: (0, 0))],
        out_specs=[pl.BlockSpec((tm, dm), lambda i, pos: (i, 0)),
                   pl.BlockSpec((tm, dm), lambda i, pos: (i, 0))],
        scratch_shapes=[pltpu.VMEM((2, MOE_TOP_K, tm, dm), F32),
                        pltpu.SemaphoreType.DMA((2,))])
    return pl.pallas_call(
        functools.partial(_moe_combine_kernel, alpha=alpha),
        grid_spec=grid_spec,
        out_shape=[jax.ShapeDtypeStruct((s, dm), F32), jax.ShapeDtypeStruct((s, dm), BF16)],
        compiler_params=_params(("arbitrary",)),
        name="moe_combine_ln",
    )(pos_km, y_sorted, x, wts, g.reshape(1, dm), b.reshape(1, dm))


def _pad_cols(w, width=LANES):
    return jnp.pad(w, ((0, 0), (0, width - w.shape[1])))


def _moe_layer(x, x_bf16, layer, w_rg, b_rg, w_re, b_re, w_gate, w_up, w_down, ln_g, ln_b, alpha, *, tm=256):
    del x_bf16
    s, dm = x.shape
    n_groups, _, n_exp = w_re.shape
    n_total = n_groups * n_exp
    w_r = _pad_cols(jnp.concatenate([w_rg, jnp.transpose(w_re, (1, 0, 2)).reshape(dm, n_total)], axis=1))
    b_r = _pad_cols(jnp.concatenate([b_rg, b_re.reshape(n_total)])[None, :])
    ids, wts, ranks, counts = moe_router(x, w_r, b_r, n_groups, n_exp)

    n_tiles = (s * MOE_TOP_K) // tm + n_total
    counts = counts[0, :n_total]
    padded = ((counts + tm - 1) // tm) * tm
    ends = jnp.cumsum(padded).astype(I32)
    starts = ends - padded
    ids2 = ids[:, :MOE_TOP_K]
    start_of_pick = jnp.sum(jnp.where(ids2[..., None] == jnp.arange(n_total, dtype=I32), starts, 0), axis=-1)
    pos = (start_of_pick + ranks[:, :MOE_TOP_K]).astype(I32)
    n_used = (ends[-1] // tm).astype(I32)
    tile_start = jnp.arange(n_tiles, dtype=I32) * tm
    tile_expert = jnp.minimum(jnp.searchsorted(ends, tile_start, side="right"), n_total - 1).astype(I32)
    last_expert = tile_expert[jnp.maximum(n_used - 1, 0)]
    tile_expert = jnp.where(jnp.arange(n_tiles) < n_used, tile_expert, last_expert)

    tile_idx = jnp.arange(n_tiles, dtype=I32)
    prev_expert = jnp.concatenate([jnp.full((1,), -1, I32), tile_expert[:-1]])
    is_first = jnp.logical_and(tile_idx < n_used, tile_expert != prev_expert)
    expert_parity = ((jnp.cumsum(is_first.astype(I32)) - 1) % 2).astype(I32)
    first_pos = jnp.where(is_first, tile_idx, n_tiles)
    next_first = jnp.concatenate([lax.cummin(first_pos, axis=0, reverse=True)[1:], jnp.full((1,), n_tiles, I32)])
    next_expert = jnp.where(next_first < n_tiles, tile_expert[jnp.minimum(next_first, n_tiles - 1)], -1).astype(I32)

    xs = moe_dispatch(x, pos.reshape(-1), ends, n_tiles, tm)
    y_sorted = moe_expert_ffn(xs, w_gate, w_up, w_down, layer, tile_expert, n_used.reshape(1), next_expert,
                              expert_parity, tm)
    pos_km = jnp.transpose(pos.reshape(s // tm, tm, MOE_TOP_K), (2, 0, 1)).reshape(-1).astype(I32)
    return moe_combine_layernorm(y_sorted, pos_km, x, wts, ln_g, ln_b, alpha, tm_pref=tm)


def kernel(x, dn_w_in, dn_conv_w, dn_A_log, dn_dt_bias, dn_norm_w, dn_w_out, fox_w_in, fox_b_f, fox_q_norm_w,
           fox_k_norm_w, fox_w_out, ln1_g, ln1_b, ln2_g, ln2_b, moe_w_rg, moe_b_rg, moe_w_re, moe_b_re,
           moe_w_gate, moe_w_up, moe_w_down):
    batch, s, dm = x.shape
    assert batch == 1
    depth = ln1_g.shape[0]
    alpha = (2 * depth) ** 0.25
    x2d = x.reshape(s, dm)
    xb = x2d.astype(BF16)
    d = HEAD_DIM
    hb = 4

    for i in range(depth):
        j = i // 2
        if i % 2 == 0:
            n_v_heads = dn_A_log.shape[1]
            qk_dim = (n_v_heads // 2) * d
            v_dim = n_v_heads * d
            conv_dim = 2 * qk_dim + v_dim
            w_in_t = jnp.transpose(dn_w_in[j])
            proj = matmul(xb, w_in_t, conv_dim + v_dim)
            hg = n_v_heads // hb
            w_b = w_in_t[conv_dim + v_dim:conv_dim + v_dim + n_v_heads].reshape(hg, hb, dm)
            w_a = w_in_t[conv_dim + v_dim + n_v_heads:].reshape(hg, hb, dm)
            w_ba = jnp.pad(jnp.concatenate([w_b, w_a], axis=1), ((0, 0), (0, LANES - 2 * hb), (0, 0)))
            ba_g = matmul(xb, w_ba.reshape(hg * LANES, dm), hg * LANES)
            zeros = jnp.zeros((hg, hb), F32)
            alog_g = jnp.pad(jnp.concatenate([zeros, dn_A_log[j].reshape(hg, hb)], axis=1),
                             ((0, 0), (0, LANES - 2 * hb))).reshape(hg, 1, LANES)
            dt_g = jnp.pad(jnp.concatenate([zeros, dn_dt_bias[j].reshape(hg, hb)], axis=1),
                           ((0, 0), (0, LANES - 2 * hb))).reshape(hg, 1, LANES)
            qkv = gdn_preprocess(proj, dn_conv_w[j], qk_dim)
            mixed = gdn_delta_rule(qkv, proj, ba_g, alog_g, dt_g, dn_norm_w[j], n_v_heads, hb=hb)
            w_out = dn_w_out[j]
        else:
            n_heads = fox_b_f.shape[1]
            dim = n_heads * d
            w_in_t = jnp.transpose(fox_w_in[j])
            proj = matmul(xb, w_in_t, 4 * dim)
            w_f = jnp.pad(w_in_t[4 * dim:], ((0, LANES - n_heads), (0, 0)))
            f_logits = matmul(xb, w_f, LANES, tn_pref=LANES)
            c_pad = fox_cumulative_gate(f_logits, _pad_cols(fox_b_f[j][None, :]))
            qa, ka, v_t = fox_preprocess(proj, c_pad, fox_q_norm_w[j], fox_k_norm_w[j], n_heads)
            mixed = fox_attention(qa, ka, v_t, proj, n_heads)
            w_out = fox_w_out[j]
        x2d, xb = matmul_residual_layernorm(mixed, w_out.astype(BF16), x2d, ln1_g[i], ln1_b[i], alpha)
        x2d, xb = _moe_layer(x2d, xb, i, moe_w_rg[i], moe_b_rg[i], moe_w_re[i], moe_b_re[i],
                             moe_w_gate, moe_w_up, moe_w_down, ln2_g[i], ln2_b[i], alpha)
    return x2d.reshape(batch, s, dm)
```

```python
import functools
import math

import jax
import jax.numpy as jnp
from jax import lax
from jax.experimental import pallas as pl
from jax.experimental.pallas import tpu as pltpu

F32 = jnp.float32
BF16 = jnp.bfloat16
I32 = jnp.int32

LANES = 128
SUBLANES = 8
V7X_VMEM_BYTES = 64 * 1024 * 1024
VMEM_LIMIT = V7X_VMEM_BYTES - 8 * 1024 * 1024

HEAD_DIM = 128
SOLVE_CHUNK = 128
DN_CONV_WIDTH = 4
MOE_GROUPS = 4
MOE_EXPERTS_PER_GROUP = 8
MOE_TOP_K = 2
NORM_EPS = 1e-6
NEG_BIG = -1e30
LOG2E = 1.4426950408889634


def _params(sem, vmem=VMEM_LIMIT):
    return pltpu.CompilerParams(dimension_semantics=sem, vmem_limit_bytes=vmem)


def _tile(dim, pref):
    t = min(dim, pref)
    assert dim % t == 0, (dim, pref)
    return t


def _softplus(x):
    return jnp.maximum(x, 0.0) + jnp.log(1.0 + jnp.exp(-jnp.abs(x)))


def _sigmoid(x):
    return 1.0 / (1.0 + jnp.exp(-x))


def _mm_kernel(a_ref, wt_ref, o_ref):
    o_ref[...] = lax.dot_general(a_ref[...], wt_ref[...].astype(BF16), (((1,), (1,)), ((), ())),
                                 preferred_element_type=F32)


def matmul(a, w_t, n_out, *, tm_pref=2048, tn_pref=512):
    m, k = a.shape
    assert a.dtype == BF16
    tm = _tile(m, tm_pref)
    tn = _tile(n_out, tn_pref)
    return pl.pallas_call(
        _mm_kernel,
        grid=(m // tm, n_out // tn),
        in_specs=[pl.BlockSpec((tm, k), lambda i, j: (i, 0)),
                  pl.BlockSpec((tn, k), lambda i, j: (j, 0))],
        out_specs=pl.BlockSpec((tm, tn), lambda i, j: (i, j)),
        out_shape=jax.ShapeDtypeStruct((m, n_out), F32),
        compiler_params=_params(("parallel", "parallel")),
        name="proj_matmul",
    )(a, w_t)


def _layernorm_rows(y, g, b):
    mu = jnp.mean(y, axis=1, keepdims=True)
    yc = y - mu
    var = jnp.mean(yc * yc, axis=1, keepdims=True)
    return yc * lax.rsqrt(var + NORM_EPS) * g + b


def _mm_res_ln_kernel(a_ref, w_ref, res_ref, g_ref, b_ref, o_ref, ob_ref, *, alpha):
    y = alpha * res_ref[...] + jnp.dot(a_ref[...], w_ref[...], preferred_element_type=F32)
    out = _layernorm_rows(y, g_ref[...], b_ref[...])
    o_ref[...] = out
    ob_ref[...] = out.astype(BF16)


def matmul_residual_layernorm(a, w_bf16, res, g, b, alpha, *, tm_pref=512):
    m, k = a.shape
    n = w_bf16.shape[1]
    tm = _tile(m, tm_pref)
    return pl.pallas_call(
        functools.partial(_mm_res_ln_kernel, alpha=alpha),
        grid=(m // tm,),
        in_specs=[pl.BlockSpec((tm, k), lambda i: (i, 0)),
                  pl.BlockSpec((k, n), lambda i: (0, 0), pipeline_mode=pl.Buffered(1)),
                  pl.BlockSpec((tm, n), lambda i: (i, 0)),
                  pl.BlockSpec((1, n), lambda i: (0, 0)),
                  pl.BlockSpec((1, n), lambda i: (0, 0))],
        out_specs=[pl.BlockSpec((tm, n), lambda i: (i, 0)),
                   pl.BlockSpec((tm, n), lambda i: (i, 0))],
        out_shape=[jax.ShapeDtypeStruct((m, n), F32), jax.ShapeDtypeStruct((m, n), BF16)],
        compiler_params=_params(("parallel",)),
        name="out_proj_res_ln",
    )(a, w_bf16, res, g.reshape(1, n), b.reshape(1, n))


def _gdn_pre_kernel(cur_ref, halo_ref, w_ref, o_ref, ext_ref, *, n_qk_blocks, q_scale):
    i = pl.program_id(0)
    j = pl.program_id(1)
    tt, cb = cur_ref.shape
    keep = (i > 0).astype(F32)
    ext_ref[0:SUBLANES, :] = halo_ref[...] * keep
    ext_ref[SUBLANES:SUBLANES + tt, :] = cur_ref[...]
    w = w_ref[...]
    acc = cur_ref[...] * w[DN_CONV_WIDTH - 1:DN_CONV_WIDTH]
    for kk in range(DN_CONV_WIDTH - 1):
        off = SUBLANES - (DN_CONV_WIDTH - 1) + kk
        acc = acc + ext_ref[off:off + tt, :] * w[kk:kk + 1]
    y = acc * _sigmoid(acc)

    def write_normed(scale):
        for h in range(cb // HEAD_DIM):
            yh = y[:, h * HEAD_DIM:(h + 1) * HEAD_DIM]
            r = lax.rsqrt(jnp.sum(yh * yh, axis=1, keepdims=True) + NORM_EPS)
            o_ref[:, h * HEAD_DIM:(h + 1) * HEAD_DIM] = yh * (r * scale if scale != 1.0 else r)

    @pl.when(j < n_qk_blocks)
    def _():
        write_normed(q_scale)

    @pl.when(jnp.logical_and(j >= n_qk_blocks, j < 2 * n_qk_blocks))
    def _():
        write_normed(1.0)

    @pl.when(j >= 2 * n_qk_blocks)
    def _():
        o_ref[...] = y


def gdn_preprocess(proj, conv_w, qk_dim, *, tt_pref=1024, cb_pref=1024):
    s = proj.shape[0]
    conv_dim = conv_w.shape[1]
    tt = _tile(s, tt_pref)
    cb = _tile(qk_dim, cb_pref)
    assert conv_dim % cb == 0 and tt % SUBLANES == 0
    hb = tt // SUBLANES
    return pl.pallas_call(
        functools.partial(_gdn_pre_kernel, n_qk_blocks=qk_dim // cb, q_scale=HEAD_DIM ** -0.5),
        grid=(s // tt, conv_dim // cb),
        in_specs=[pl.BlockSpec((tt, cb), lambda i, j: (i, j)),
                  pl.BlockSpec((SUBLANES, cb), lambda i, j: (jnp.maximum(i * hb - 1, 0), j)),
                  pl.BlockSpec((DN_CONV_WIDTH, cb), lambda i, j: (0, j))],
        out_specs=pl.BlockSpec((tt, cb), lambda i, j: (i, j)),
        out_shape=jax.ShapeDtypeStruct((s, conv_dim), F32),
        scratch_shapes=[pltpu.VMEM((tt + SUBLANES, cb), F32)],
        compiler_params=_params(("parallel", "parallel")),
        name="gdn_conv_norm",
    )(proj, proj, conv_w)


def _bmm(a, b):
    return jnp.einsum("nij,njk->nik", a.astype(BF16), b.astype(BF16), preferred_element_type=F32)


def _bmm_nt(a, b):
    return jnp.einsum("nid,njd->nij", a.astype(BF16), b.astype(BF16), preferred_element_type=F32)


def _chunk_cumsum(x, chunk, reverse=False):
    rows = x.shape[0]
    pos = lax.broadcasted_iota(I32, x.shape, 0) % chunk
    step = 1
    while step < chunk:
        if reverse:
            shifted = pltpu.roll(x, rows - step, axis=0)
            x = x + jnp.where(pos + step < chunk, shifted, 0.0)
        else:
            shifted = pltpu.roll(x, step, axis=0)
            x = x + jnp.where(pos >= step, shifted, 0.0)
        step *= 2
    return x


def _cat_lanes(a, b):
    return jnp.concatenate([a, b], axis=-1)


def _block_diag(a, b):
    return jnp.concatenate([_cat_lanes(a, jnp.zeros_like(b)), _cat_lanes(jnp.zeros_like(a), b)], axis=-2)


def _unit_lower_inverse_pair(lower_a, lower_b, eye):
    c = lower_a.shape[-1]
    pa, pb = lower_a.astype(BF16), lower_b.astype(BF16)
    tinv = _cat_lanes(eye - lower_a, eye - lower_b)
    span = 2
    while span < c:
        power = jnp.einsum("nij,njk->nik", _cat_lanes(pa, pb), _block_diag(pa, pb), preferred_element_type=F32)
        pa, pb = power[..., :c].astype(BF16), power[..., c:].astype(BF16)
        tinv = tinv + jnp.einsum("nij,njk->nik", tinv.astype(BF16), _block_diag(pa, pb),
                                 preferred_element_type=F32)
        span *= 2
    return tinv[..., :c], tinv[..., c:]


def _delta_kernel(q_ref, k_ref, v_ref, z_ref, ba_ref, alog_ref, dt_ref, nw_ref, o_ref,
                  state_ref, u_ref, wq_ref, ak_ref, egl_ref, osc_ref, *, hb):
    t = pl.program_id(1)
    tb = q_ref.shape[0]
    c = SOLVE_CHUNK
    nc = tb // c
    d = HEAD_DIM

    @pl.when(t == 0)
    def _():
        state_ref[...] = jnp.zeros_like(state_ref)

    group = pl.program_id(0)
    ba = pltpu.roll(ba_ref[...], (LANES - group * (2 * hb)) % LANES, axis=1)
    beta_all = _sigmoid(ba)
    g_all = -jnp.exp(alog_ref[...]) * _softplus(ba + dt_ref[...])
    big_g = _chunk_cumsum(g_all, c)
    tail_g = _chunk_cumsum(g_all, c, reverse=True) - g_all
    e_g = jnp.exp(big_g)
    e_tail = jnp.exp(tail_g)
    e_tot = jnp.exp(big_g + tail_g)
    big_g_t = big_g.T

    row = lax.broadcasted_iota(I32, (c, c), 0)
    col = lax.broadcasted_iota(I32, (c, c), 1)
    causal = row >= col
    strict = row > col
    eye = (row == col).astype(F32)[None]

    def head_terms(hh):
        gi = hb + hh
        gcol_b = jnp.broadcast_to(big_g[:, gi:gi + 1], (tb, c))
        decays = []
        for n in range(nc):
            diff = gcol_b[n * c:(n + 1) * c, :] - big_g_t[gi:gi + 1, n * c:(n + 1) * c]
            decays.append(jnp.where(causal, jnp.exp(jnp.where(causal, diff, 0.0)), 0.0))
        return beta_all[:, hh:hh + 1], e_g[:, gi:gi + 1], e_tail[:, gi:gi + 1], jnp.stack(decays, axis=0)

    for p in range(hb // 2):
        heads = (2 * p, 2 * p + 1)
        qh = q_ref[:, p * d:(p + 1) * d]
        kh = k_ref[:, p * d:(p + 1) * d]
        terms = [head_terms(hh) for hh in heads]
        kbs = [kh * tm_[0] for tm_ in terms]
        stacked = jnp.concatenate([kbs[0].reshape(nc, c, d), kbs[1].reshape(nc, c, d), qh.reshape(nc, c, d)],
                                  axis=1)
        prod = _bmm_nt(stacked, kh.reshape(nc, c, d))
        qk = prod[:, 2 * c:]
        lowers = [jnp.where(strict[None], prod[:, i * c:(i + 1) * c] * terms[i][3], 0.0) for i in range(2)]
        tinvs = _unit_lower_inverse_pair(lowers[0], lowers[1], eye)
        packed = []
        for i, hh in enumerate(heads):
            bcol, egcol, etcol, decay = terms[i]
            vh = v_ref[:, hh * d:(hh + 1) * d]
            rhs = _cat_lanes((vh * bcol).reshape(nc, c, d), (kbs[i] * egcol).reshape(nc, c, d))
            sol = _bmm(tinvs[i], rhs)
            u_ref[hh] = sol[..., :d]
            wq_i = jnp.concatenate([sol[..., d:], (qh * egcol).reshape(nc, c, d)], axis=1).astype(BF16)
            kd_t = (kh * etcol).T.astype(BF16)
            kdt3 = jnp.stack([kd_t[:, n * c:(n + 1) * c] for n in range(nc)], axis=0)
            packed.append((wq_i, jnp.concatenate([(qk * decay).astype(BF16), kdt3], axis=1)))
            egl_ref[hh] = jnp.broadcast_to(e_tot[:, hb + hh:hb + hh + 1], (tb, d))
        wq_ref[p] = _cat_lanes(packed[0][0], packed[1][0])
        ak_ref[p] = _cat_lanes(packed[0][1], packed[1][1])

    states = [state_ref[hh] for hh in range(hb)]
    for n in range(nc):
        r0 = n * c
        for p in range(hb // 2):
            ha, hb_ = 2 * p, 2 * p + 1
            sa, sb = states[ha], states[hb_]
            ws = jnp.dot(wq_ref[p, n], _block_diag(sa.astype(BF16), sb.astype(BF16)),
                         preferred_element_type=F32)
            vna = u_ref[ha, n] - ws[:c, :d]
            vnb = u_ref[hb_, n] - ws[:c, d:]
            r = jnp.dot(ak_ref[p, n], _block_diag(vna.astype(BF16), vnb.astype(BF16)),
                        preferred_element_type=F32)
            osc_ref[r0:r0 + c, ha * d:(ha + 1) * d] = ws[c:, :d] + r[:c, :d]
            osc_ref[r0:r0 + c, hb_ * d:(hb_ + 1) * d] = ws[c:, d:] + r[:c, d:]
            states[ha] = sa * egl_ref[ha, r0:r0 + 1, :] + r[c:, :d]
            states[hb_] = sb * egl_ref[hb_, r0:r0 + 1, :] + r[c:, d:]
    for hh in range(hb):
        state_ref[hh] = states[hh]

    nw = nw_ref[...]
    for hh in range(hb):
        o = osc_ref[:, hh * d:(hh + 1) * d]
        z = z_ref[:, hh * d:(hh + 1) * d]
        ms = jnp.mean(o * o, axis=1, keepdims=True)
        y = o * lax.rsqrt(ms + NORM_EPS) * nw * (z * _sigmoid(z))
        o_ref[:, hh * d:(hh + 1) * d] = y.astype(BF16)


def gdn_delta_rule(qkv, proj, ba_g, alog_g, dt_g, norm_w, n_v_heads, *, hb, tb_pref=1024):
    s = qkv.shape[0]
    d = HEAD_DIM
    c = SOLVE_CHUNK
    qk_dim = (n_v_heads // 2) * d
    v_dim = n_v_heads * d
    tb = _tile(s, tb_pref)
    nc = tb // c
    hg = n_v_heads // hb
    qw = (hb // 2) * d
    vw = hb * d
    return pl.pallas_call(
        functools.partial(_delta_kernel, hb=hb),
        grid=(hg, s // tb),
        in_specs=[pl.BlockSpec((tb, qw), lambda g, t: (t, g)),
                  pl.BlockSpec((tb, qw), lambda g, t: (t, qk_dim // qw + g)),
                  pl.BlockSpec((tb, vw), lambda g, t: (t, 2 * qk_dim // vw + g)),
                  pl.BlockSpec((tb, vw), lambda g, t: (t, (2 * qk_dim + v_dim) // vw + g)),
                  pl.BlockSpec((tb, LANES), lambda g, t: (t, 0)),
                  pl.BlockSpec((None, 1, LANES), lambda g, t: (g, 0, 0)),
                  pl.BlockSpec((None, 1, LANES), lambda g, t: (g, 0, 0)),
                  pl.BlockSpec((1, d), lambda g, t: (0, 0))],
        out_specs=pl.BlockSpec((tb, vw), lambda g, t: (t, g)),
        out_shape=jax.ShapeDtypeStruct((s, v_dim), BF16),
        scratch_shapes=[pltpu.VMEM((hb, d, d), F32),
                        pltpu.VMEM((hb, nc, c, d), F32),
                        pltpu.VMEM((hb // 2, nc, 2 * c, 2 * d), BF16),
                        pltpu.VMEM((hb // 2, nc, c + d, 2 * c), BF16),
                        pltpu.VMEM((hb, tb, d), F32),
                        pltpu.VMEM((tb, vw), F32)],
        compiler_params=_params(("parallel", "arbitrary")),
        name="gdn_delta_rule",
    )(qkv, qkv, qkv, proj, ba_g, alog_g, dt_g, norm_w.reshape(1, d))


def _split3_bf16(x):
    hi = x.astype(BF16).astype(F32)
    r1 = x - hi
    mid = r1.astype(BF16).astype(F32)
    return hi, mid, r1 - mid


def _fox_pre_kernel(q_ref, k_ref, v_ref, c_ref, qw_ref, kw_ref, qa_ref, ka_ref, vt_ref, *, n_heads):
    d = HEAD_DIM
    tt = q_ref.shape[0]
    c2 = c_ref[...] * LOG2E
    lane = lax.broadcasted_iota(I32, (tt, d), 1)
    qw = qw_ref[...] * (d ** -0.5 * LOG2E)
    kw = kw_ref[...]
    for h in range(n_heads):
        qh = q_ref[:, h * d:(h + 1) * d]
        kh = k_ref[:, h * d:(h + 1) * d]
        qn = qh * lax.rsqrt(jnp.mean(qh * qh, axis=1, keepdims=True) + NORM_EPS) * qw
        kn = kh * lax.rsqrt(jnp.mean(kh * kh, axis=1, keepdims=True) + NORM_EPS) * kw
        hi, mid, lo = _split3_bf16(c2[:, h:h + 1])
        aq = jnp.where(lane == 0, hi, jnp.where(lane == 1, mid, jnp.where(lane == 2, lo,
                                                                          jnp.where(lane < 6, 1.0, 0.0))))
        ak = jnp.where(lane < 3, 1.0, jnp.where(lane == 3, -hi, jnp.where(lane == 4, -mid,
                                                                           jnp.where(lane == 5, -lo, 0.0))))
        qa_ref[:, 2 * h * d:(2 * h + 1) * d] = qn.astype(BF16)
        qa_ref[:, (2 * h + 1) * d:(2 * h + 2) * d] = aq.astype(BF16)
        ka_ref[:, 2 * h * d:(2 * h + 1) * d] = kn.astype(BF16)
        ka_ref[:, (2 * h + 1) * d:(2 * h + 2) * d] = ak.astype(BF16)
    vt_ref[...] = v_ref[...].T.astype(BF16)


def fox_preprocess(proj, c_pad, q_norm_w, k_norm_w, n_heads, *, tt_pref=256):
    s = proj.shape[0]
    d = HEAD_DIM
    dim = n_heads * d
    tt = _tile(s, tt_pref)
    return pl.pallas_call(
        functools.partial(_fox_pre_kernel, n_heads=n_heads),
        grid=(s // tt,),
        in_specs=[pl.BlockSpec((tt, dim), lambda i: (i, 0)),
                  pl.BlockSpec((tt, dim), lambda i: (i, 1)),
                  pl.BlockSpec((tt, dim), lambda i: (i, 2)),
                  pl.BlockSpec((tt, LANES), lambda i: (i, 0)),
                  pl.BlockSpec((1, d), lambda i: (0, 0)),
                  pl.BlockSpec((1, d), lambda i: (0, 0))],
        out_specs=[pl.BlockSpec((tt, 2 * dim), lambda i: (i, 0)),
                   pl.BlockSpec((tt, 2 * dim), lambda i: (i, 0)),
                   pl.BlockSpec((dim, tt), lambda i: (0, i))],
        out_shape=[jax.ShapeDtypeStruct((s, 2 * dim), BF16), jax.ShapeDtypeStruct((s, 2 * dim), BF16),
                   jax.ShapeDtypeStruct((dim, s), BF16)],
        compiler_params=_params(("parallel",)),
        name="fox_qkv_prep",
    )(proj, proj, proj, c_pad, q_norm_w.reshape(1, d), k_norm_w.reshape(1, d))


def _fox_cumgate_kernel(f_ref, b_ref, o_ref, carry_ref):
    i = pl.program_id(0)
    tt = f_ref.shape[0]

    @pl.when(i == 0)
    def _():
        carry_ref[...] = jnp.zeros_like(carry_ref)

    x = f_ref[...] + b_ref[...]
    log_f = jnp.minimum(x, 0.0) - jnp.log(1.0 + jnp.exp(-jnp.abs(x)))
    csum = _chunk_cumsum(log_f, tt) + carry_ref[...]
    o_ref[...] = csum
    carry_ref[...] = csum[tt - 1:tt, :]


def fox_cumulative_gate(f_logits, b_row, *, tt_pref=512):
    s = f_logits.shape[0]
    tt = _tile(s, tt_pref)
    return pl.pallas_call(
        _fox_cumgate_kernel,
        grid=(s // tt,),
        in_specs=[pl.BlockSpec((tt, LANES), lambda i: (i, 0)),
                  pl.BlockSpec((1, LANES), lambda i: (0, 0))],
        out_specs=pl.BlockSpec((tt, LANES), lambda i: (i, 0)),
        out_shape=jax.ShapeDtypeStruct((s, LANES), F32),
        scratch_shapes=[pltpu.VMEM((1, LANES), F32)],
        compiler_params=_params(("arbitrary",)),
        name="fox_cumulative_gate",
    )(f_logits, b_row)


def _fox_attn_kernel(q_ref, k_ref, vt_ref, gate_ref, o_ref, m_ref, l_ref, acc_ref):
    qi = pl.program_id(1)
    tq = q_ref.shape[0]
    tk = tq
    q = q_ref[...]

    m_ref[...] = jnp.full_like(m_ref, NEG_BIG)
    l_ref[...] = jnp.zeros_like(l_ref)
    acc_ref[...] = jnp.zeros_like(acc_ref)

    def block(kj, masked, state):
        m_old, l_old, acc_old = state
        k0 = pl.multiple_of(kj * tk, tk)
        kb = k_ref[pl.ds(k0, tk), :]
        st = lax.dot_general(kb, q, (((1,), (1,)), ((), ())), preferred_element_type=F32)
        if masked:
            kpos = lax.broadcasted_iota(I32, (tk, tq), 0)
            qpos = lax.broadcasted_iota(I32, (tk, tq), 1)
            st = jnp.where(kpos <= qpos, st, -jnp.inf)
        m_new = jnp.maximum(m_old, jnp.max(st, axis=0, keepdims=True))
        alpha = jnp.exp2(m_old - m_new)
        p = jnp.exp2(st - m_new)
        l_new = alpha * l_old + jnp.sum(p, axis=0, keepdims=True)
        vb = vt_ref[:, pl.ds(k0, tk)]
        acc_new = alpha * acc_old + jnp.dot(vb, p.astype(BF16), preferred_element_type=F32)
        return m_new, l_new, acc_new

    def load_state():
        return m_ref[...], l_ref[...], acc_ref[...]

    def store_state(state):
        m_ref[...], l_ref[...], acc_ref[...] = state

    def full_body(kj, carry):
        store_state(block(kj, False, load_state()))
        return carry

    lax.fori_loop(0, qi, full_body, 0)
    _, l_fin, acc_fin = block(qi, True, load_state())

    o_t = acc_fin / l_fin
    o_ref[...] = (o_t.T * _sigmoid(gate_ref[...])).astype(BF16)


def fox_attention(qa, ka, v_t, proj, n_heads, *, tq_pref=1024):
    s = qa.shape[0]
    d = HEAD_DIM
    tq = _tile(s, tq_pref)
    return pl.pallas_call(
        _fox_attn_kernel,
        grid=(n_heads, s // tq),
        in_specs=[pl.BlockSpec((tq, 2 * d), lambda h, i: (i, h)),
                  pl.BlockSpec((s, 2 * d), lambda h, i: (0, h)),
                  pl.BlockSpec((d, s), lambda h, i: (h, 0)),
                  pl.BlockSpec((tq, d), lambda h, i: (i, 3 * n_heads + h))],
        out_specs=pl.BlockSpec((tq, d), lambda h, i: (i, h)),
        out_shape=jax.ShapeDtypeStruct((s, n_heads * d), BF16),
        scratch_shapes=[pltpu.VMEM((1, tq), F32), pltpu.VMEM((1, tq), F32), pltpu.VMEM((d, tq), F32)],
        compiler_params=_params(("parallel", "arbitrary")),
        name="fox_attention",
    )(qa, ka, v_t, proj)


def _split_bf16(x):
    hi = x.astype(BF16)
    lo = (x - hi.astype(F32)).astype(BF16)
    return hi, lo


def _first_argmax(vals, mask, lane_f):
    v = jnp.where(mask, vals, NEG_BIG)
    vmax = jnp.max(v, axis=1, keepdims=True)
    hit = jnp.logical_and(mask, v == vmax)
    idx = jnp.min(jnp.where(hit, lane_f, float(LANES)), axis=1, keepdims=True)
    return vmax, idx


def _router_kernel(x_ref, w_ref, b_ref, ids_ref, wts_ref, rank_ref, cnt_ref, carry_ref, *, n_groups, n_exp):
    i = pl.program_id(0)
    tm = x_ref.shape[0]

    @pl.when(i == 0)
    def _():
        carry_ref[...] = jnp.zeros_like(carry_ref)

    xh, xl = _split_bf16(x_ref[...])
    wh, wl = _split_bf16(w_ref[...])
    lg = (jnp.dot(xh, wh, preferred_element_type=F32) + jnp.dot(xh, wl, preferred_element_type=F32)
          + jnp.dot(xl, wh, preferred_element_type=F32)) + b_ref[...]
    lane = lax.broadcasted_iota(I32, lg.shape, 1)
    lane_f = lane.astype(F32)

    gmask = lane < n_groups
    gmax, gidx = _first_argmax(lg, gmask, lane_f)
    gsum = jnp.sum(jnp.where(gmask, jnp.exp(jnp.where(gmask, lg, NEG_BIG) - gmax), 0.0), axis=1, keepdims=True)
    p_group = 1.0 / gsum

    e_lo = float(n_groups) + gidx * float(n_exp)
    emask = jnp.logical_and(lane_f >= e_lo, lane_f < e_lo + float(n_exp))
    t1, i1 = _first_argmax(lg, emask, lane_f)
    emask2 = jnp.logical_and(emask, lane_f != i1)
    t2, i2 = _first_argmax(lg, emask2, lane_f)
    e21 = jnp.exp(t2 - t1)
    p1 = p_group / (1.0 + e21)
    p2 = p_group * e21 / (1.0 + e21)
    id1 = i1 - float(n_groups)
    id2 = i2 - float(n_groups)

    oh = jnp.where(lane_f == id1, 1.0, 0.0) + jnp.where(lane_f == id2, 1.0, 0.0)
    r = lax.broadcasted_iota(I32, (tm, tm), 0)
    cc = lax.broadcasted_iota(I32, (tm, tm), 1)
    tril = jnp.where(r > cc, 1.0, 0.0).astype(BF16)
    before = jnp.dot(tril, oh.astype(BF16), preferred_element_type=F32) + carry_ref[...]
    rank1 = jnp.sum(jnp.where(lane_f == id1, before, 0.0), axis=1, keepdims=True)
    rank2 = jnp.sum(jnp.where(lane_f == id2, before, 0.0), axis=1, keepdims=True)
    new_carry = carry_ref[...] + jnp.sum(oh, axis=0, keepdims=True)
    carry_ref[...] = new_carry
    cnt_ref[...] = jnp.broadcast_to(new_carry, cnt_ref.shape).astype(I32)

    first = lane == 0
    second = lane == 1
    ids_ref[...] = jnp.where(first, id1, jnp.where(second, id2, 0.0)).astype(I32)
    wts_ref[...] = jnp.where(first, p1, jnp.where(second, p2, 0.0))
    rank_ref[...] = jnp.where(first, rank1, jnp.where(second, rank2, 0.0)).astype(I32)


def moe_router(x, w_pad, b_pad, n_groups, n_exp, *, tm_pref=512):
    s, dm = x.shape
    tm = _tile(s, tm_pref)
    return pl.pallas_call(
        functools.partial(_router_kernel, n_groups=n_groups, n_exp=n_exp),
        grid=(s // tm,),
        in_specs=[pl.BlockSpec((tm, dm), lambda i: (i, 0)),
                  pl.BlockSpec((dm, LANES), lambda i: (0, 0)),
                  pl.BlockSpec((1, LANES), lambda i: (0, 0))],
        out_specs=[pl.BlockSpec((tm, LANES), lambda i: (i, 0)),
                   pl.BlockSpec((tm, LANES), lambda i: (i, 0)),
                   pl.BlockSpec((tm, LANES), lambda i: (i, 0)),
                   pl.BlockSpec((SUBLANES, LANES), lambda i: (0, 0))],
        out_shape=[jax.ShapeDtypeStruct((s, LANES), I32), jax.ShapeDtypeStruct((s, LANES), F32),
                   jax.ShapeDtypeStruct((s, LANES), I32), jax.ShapeDtypeStruct((SUBLANES, LANES), I32)],
        scratch_shapes=[pltpu.VMEM((1, LANES), F32)],
        compiler_params=_params(("arbitrary",)),
        name="moe_router",
    )(x, w_pad, b_pad)


ROW_DMA_PRIORITIES = (0, 1)


def _start_row_gather(src_hbm, dst, sem, idx_ref, base):
    for r in range(dst.shape[0]):
        tok = idx_ref[base + r]
        pltpu.make_async_copy(src_hbm.at[pl.ds(tok, 1), :], dst.at[pl.ds(r, 1), :], sem).start(
            priority=ROW_DMA_PRIORITIES[r % 2])


def _wait_row_gather(dst, sem):
    pltpu.make_async_copy(dst, dst, sem).wait()


def _moe_dispatch_kernel(pos_ref, ends_ref, x_ref, xs_hbm, zbuf, zsem, sem):
    i = pl.program_id(0)
    tm = x_ref.shape[0]
    n_exp = ends_ref.shape[0]

    @pl.when(i == 0)
    def _():
        zbuf[...] = jnp.zeros_like(zbuf)

        def last_tile_copy(e):
            start = pl.multiple_of(ends_ref[e] - tm, tm)
            return pltpu.make_async_copy(zbuf, xs_hbm.at[pl.ds(start, tm), :], zsem)

        def non_empty(e):
            return ends_ref[e] > (ends_ref[e - 1] if e > 0 else 0)

        def tail_tile_copy(t):
            return pltpu.make_async_copy(zbuf, xs_hbm.at[pl.ds(pl.multiple_of(t * tm, tm), tm), :], zsem)

        n_used = ends_ref[n_exp - 1] // tm
        n_tiles = xs_hbm.shape[0] // tm
        for e in range(n_exp):
            pl.when(non_empty(e))(lambda e=e: last_tile_copy(e).start())
        lax.fori_loop(n_used, n_tiles, lambda t, c: (tail_tile_copy(t).start(), c)[1], 0)
        for e in range(n_exp):
            pl.when(non_empty(e))(lambda e=e: last_tile_copy(e).wait())
        lax.fori_loop(n_used, n_tiles, lambda t, c: (tail_tile_copy(t).wait(), c)[1], 0)

    for r in range(tm):
        for kk in range(MOE_TOP_K):
            slot = pos_ref[(i * tm + r) * MOE_TOP_K + kk]
            pltpu.make_async_copy(x_ref.at[pl.ds(r, 1), :], xs_hbm.at[pl.ds(slot, 1), :], sem).start(
                priority=ROW_DMA_PRIORITIES[kk])
    for kk in range(MOE_TOP_K):
        pltpu.make_async_copy(x_ref, x_ref, sem).wait()


def moe_dispatch(x, pos_flat, ends, n_tiles, tm):
    s, dm = x.shape
    grid_spec = pltpu.PrefetchScalarGridSpec(
        num_scalar_prefetch=2,
        grid=(s // tm,),
        in_specs=[pl.BlockSpec((tm, dm), lambda i, pos, ends: (i, 0))],
        out_specs=pl.BlockSpec(memory_space=pl.ANY),
        scratch_shapes=[pltpu.VMEM((tm, dm), F32),
                        pltpu.SemaphoreType.DMA(()),
                        pltpu.SemaphoreType.DMA(())])
    return pl.pallas_call(
        _moe_dispatch_kernel,
        grid_spec=grid_spec,
        out_shape=jax.ShapeDtypeStruct((n_tiles * tm, dm), F32),
        compiler_params=_params(("arbitrary",)),
        name="moe_dispatch",
    )(pos_flat, ends, x)


def _moe_ffn_kernel(te_ref, nu_ref, nxt_ref, par_ref, x_ref, wg_hbm, wu_hbm, wd_hbm, o_ref,
                    wg_buf, wu_buf, wd_buf, wsem, wgub, wdb, *, base):
    i = pl.program_id(0)
    n_used = nu_ref[0]
    fe = wg_buf.shape[2]

    def weight_copies(expert, slot):
        e = base + expert
        return (pltpu.make_async_copy(wg_hbm.at[e], wg_buf.at[slot], wsem.at[slot]),
                pltpu.make_async_copy(wu_hbm.at[e], wu_buf.at[slot], wsem.at[slot]),
                pltpu.make_async_copy(wd_hbm.at[e], wd_buf.at[slot], wsem.at[slot]))

    @pl.when(i < n_used)
    def _():
        is_new = jnp.logical_or(i == 0, te_ref[i] != te_ref[jnp.maximum(i - 1, 0)])

        @pl.when(is_new)
        def _():
            slot = par_ref[i]

            @pl.when(i == 0)
            def _():
                for cp in weight_copies(te_ref[0], 0):
                    cp.start()

            for cp in weight_copies(te_ref[i], slot):
                cp.wait()

            @pl.when(nxt_ref[i] >= 0)
            def _():
                for cp in weight_copies(nxt_ref[i], 1 - slot):
                    cp.start()

            wgub[:, :fe] = wg_buf[slot].astype(BF16)
            wgub[:, fe:] = wu_buf[slot].astype(BF16)
            wdb[...] = wd_buf[slot].astype(BF16)

        x = x_ref[...].astype(BF16)
        gate_up = jnp.dot(x, wgub[...], preferred_element_type=F32)
        gate, up = gate_up[:, :fe], gate_up[:, fe:]
        hmid = (gate * _sigmoid(gate) * up).astype(BF16)
        o_ref[...] = jnp.dot(hmid, wdb[...], preferred_element_type=F32)

    @pl.when(i >= n_used)
    def _():
        o_ref[...] = jnp.zeros_like(o_ref)


def moe_expert_ffn(xs, w_gate, w_up, w_down, layer, tile_expert, n_used, next_expert, expert_parity, tm):
    dm = xs.shape[1]
    n_exp_total = MOE_GROUPS * MOE_EXPERTS_PER_GROUP
    fe = w_gate.shape[-1]
    n_tiles = tile_expert.shape[0]
    wg = w_gate.reshape(-1, dm, fe)
    wu = w_up.reshape(-1, dm, fe)
    wd = w_down.reshape(-1, fe, dm)
    grid_spec = pltpu.PrefetchScalarGridSpec(
        num_scalar_prefetch=4,
        grid=(n_tiles,),
        in_specs=[pl.BlockSpec((tm, dm), lambda i, te, nu, nxt, par: (jnp.minimum(i, nu[0] - 1), 0)),
                  pl.BlockSpec(memory_space=pl.ANY),
                  pl.BlockSpec(memory_space=pl.ANY),
                  pl.BlockSpec(memory_space=pl.ANY)],
        out_specs=pl.BlockSpec((tm, dm), lambda i, te, nu, nxt, par: (i, 0)),
        scratch_shapes=[pltpu.VMEM((2, dm, fe), F32),
                        pltpu.VMEM((2, dm, fe), F32),
                        pltpu.VMEM((2, fe, dm), F32),
                        pltpu.SemaphoreType.DMA((2,)),
                        pltpu.VMEM((dm, 2 * fe), BF16),
                        pltpu.VMEM((fe, dm), BF16)])
    return pl.pallas_call(
        functools.partial(_moe_ffn_kernel, base=layer * n_exp_total),
        grid_spec=grid_spec,
        out_shape=jax.ShapeDtypeStruct((n_tiles * tm, dm), F32),
        compiler_params=_params(("arbitrary",)),
        name="moe_expert_ffn",
    )(tile_expert, n_used, next_expert, expert_parity, xs, wg, wu, wd)


def _moe_combine_kernel(pos_ref, y_hbm, x_ref, wts_ref, g_ref, b_ref, o_ref, ob_ref, ybuf, sem, *, alpha):
    i = pl.program_id(0)
    n = pl.num_programs(0)
    tm = x_ref.shape[0]

    def gather(tile, sl):
        for kk in range(MOE_TOP_K):
            _start_row_gather(y_hbm, ybuf.at[sl, kk], sem.at[sl], pos_ref, (kk * n + tile) * tm)

    @pl.when(i == 0)
    def _():
        gather(0, 0)

    def tile_body(slot, prefetch):
        _wait_row_gather(ybuf.at[slot], sem.at[slot])
        if prefetch:
            gather(i + 1, 1 - slot)
        wts = wts_ref[...]
        mix = wts[:, 0:1] * ybuf[slot, 0] + wts[:, 1:2] * ybuf[slot, 1]
        y = alpha * x_ref[...] + mix
        out = _layernorm_rows(y, g_ref[...], b_ref[...])
        o_ref[...] = out
        ob_ref[...] = out.astype(BF16)

    for slot in range(2):
        pl.when(jnp.logical_and(i + 1 < n, i % 2 == slot))(functools.partial(tile_body, slot, True))
        pl.when(jnp.logical_and(i + 1 == n, i % 2 == slot))(functools.partial(tile_body, slot, False))


def moe_combine_layernorm(y_sorted, pos_km, x, wts, g, b, alpha, *, tm_pref=256):
    s, dm = x.shape
    tm = _tile(s, tm_pref)
    grid_spec = pltpu.PrefetchScalarGridSpec(
        num_scalar_prefetch=1,
        grid=(s // tm,),
        in_specs=[pl.BlockSpec(memory_space=pl.ANY),
                  pl.BlockSpec((tm, dm), lambda i, pos: (i, 0)),
                  pl.BlockSpec((tm, LANES), lambda i, pos: (i, 0)),
                  pl.BlockSpec((1, dm), lambda i, pos: (0, 0)),
                  pl.BlockSpec((1, dm), lambda i, pos: (0, 0))],
        out_specs=[pl.BlockSpec((tm, dm), lambda i, pos: (i, 0)),
                   pl.BlockSpec((tm, dm), lambda i, pos: (i, 0))],
        scratch_shapes=[pltpu.VMEM((2, MOE_TOP_K, tm, dm), F32),
                        pltpu.SemaphoreType.DMA((2,))])
    return pl.pallas_call(
        functools.partial(_moe_combine_kernel, alpha=alpha),
        grid_spec=grid_spec,
        out_shape=[jax.ShapeDtypeStruct((s, dm), F32), jax.ShapeDtypeStruct((s, dm), BF16)],
        compiler_params=_params(("arbitrary",)),
        name="moe_combine_ln",
    )(pos_km, y_sorted, x, wts, g.reshape(1, dm), b.reshape(1, dm))


def _pad_cols(w, width=LANES):
    return jnp.pad(w, ((0, 0), (0, width - w.shape[1])))


def _moe_layer(x, x_bf16, layer, w_rg, b_rg, w_re, b_re, w_gate, w_up, w_down, ln_g, ln_b, alpha, *, tm=256):
    del x_bf16
    s, dm = x.shape
    n_groups, _, n_exp = w_re.shape
    n_total = n_groups * n_exp
    w_r = _pad_cols(jnp.concatenate([w_rg, jnp.transpose(w_re, (1, 0, 2)).reshape(dm, n_total)], axis=1))
    b_r = _pad_cols(jnp.concatenate([b_rg, b_re.reshape(n_total)])[None, :])
    ids, wts, ranks, counts = moe_router(x, w_r, b_r, n_groups, n_exp)

    n_tiles = (s * MOE_TOP_K) // tm + n_total
    counts = counts[0, :n_total]
    padded = ((counts + tm - 1) // tm) * tm
    ends = jnp.cumsum(padded).astype(I32)
    starts = ends - padded
    ids2 = ids[:, :MOE_TOP_K]
    start_of_pick = jnp.sum(jnp.where(ids2[..., None] == jnp.arange(n_total, dtype=I32), starts, 0), axis=-1)
    pos = (start_of_pick + ranks[:, :MOE_TOP_K]).astype(I32)
    n_used = (ends[-1] // tm).astype(I32)
    tile_start = jnp.arange(n_tiles, dtype=I32) * tm
    tile_expert = jnp.minimum(jnp.searchsorted(ends, tile_start, side="right"), n_total - 1).astype(I32)
    last_expert = tile_expert[jnp.maximum(n_used - 1, 0)]
    tile_expert = jnp.where(jnp.arange(n_tiles) < n_used, tile_expert, last_expert)

    tile_idx = jnp.arange(n_tiles, dtype=I32)
    prev_expert = jnp.concatenate([jnp.full((1,), -1, I32), tile_expert[:-1]])
    is_first = jnp.logical_and(tile_idx < n_used, tile_expert != prev_expert)
    expert_parity = ((jnp.cumsum(is_first.astype(I32)) - 1) % 2).astype(I32)
    first_pos = jnp.where(is_first, tile_idx, n_tiles)
    next_first = jnp.concatenate([lax.cummin(first_pos, axis=0, reverse=True)[1:], jnp.full((1,), n_tiles, I32)])
    next_expert = jnp.where(next_first < n_tiles, tile_expert[jnp.minimum(next_first, n_tiles - 1)], -1).astype(I32)

    xs = moe_dispatch(x, pos.reshape(-1), ends, n_tiles, tm)
    y_sorted = moe_expert_ffn(xs, w_gate, w_up, w_down, layer, tile_expert, n_used.reshape(1), next_expert,
                              expert_parity, tm)
    pos_km = jnp.transpose(pos.reshape(s // tm, tm, MOE_TOP_K), (2, 0, 1)).reshape(-1).astype(I32)
    return moe_combine_layernorm(y_sorted, pos_km, x, wts, ln_g, ln_b, alpha, tm_pref=tm)


def kernel(x, dn_w_in, dn_conv_w, dn_A_log, dn_dt_bias, dn_norm_w, dn_w_out, fox_w_in, fox_b_f, fox_q_norm_w,
           fox_k_norm_w, fox_w_out, ln1_g, ln1_b, ln2_g, ln2_b, moe_w_rg, moe_b_rg, moe_w_re, moe_b_re,
           moe_w_gate, moe_w_up, moe_w_down):
    batch, s, dm = x.shape
    assert batch == 1
    depth = ln1_g.shape[0]
    alpha = (2 * depth) ** 0.25
    x2d = x.reshape(s, dm)
    xb = x2d.astype(BF16)
    d = HEAD_DIM
    hb = 4

    for i in range(depth):
        j = i // 2
        if i % 2 == 0:
            n_v_heads = dn_A_log.shape[1]
            qk_dim = (n_v_heads // 2) * d
            v_dim = n_v_heads * d
            conv_dim = 2 * qk_dim + v_dim
            w_in_t = jnp.transpose(dn_w_in[j])
            proj = matmul(xb, w_in_t, conv_dim + v_dim)
            hg = n_v_heads // hb
            assert 2 * n_v_heads <= LANES
            w_b = w_in_t[conv_dim + v_dim:conv_dim + v_dim + n_v_heads].reshape(hg, hb, dm)
            w_a = w_in_t[conv_dim + v_dim + n_v_heads:].reshape(hg, hb, dm)
            w_ba = jnp.concatenate([w_b, w_a], axis=1).reshape(2 * n_v_heads, dm)
            ba_g = matmul(xb, jnp.pad(w_ba, ((0, LANES - 2 * n_v_heads), (0, 0))), LANES, tn_pref=LANES)
            zeros = jnp.zeros((hg, hb), F32)
            alog_g = jnp.pad(jnp.concatenate([zeros, dn_A_log[j].reshape(hg, hb)], axis=1),
                             ((0, 0), (0, LANES - 2 * hb))).reshape(hg, 1, LANES)
            dt_g = jnp.pad(jnp.concatenate([zeros, dn_dt_bias[j].reshape(hg, hb)], axis=1),
                           ((0, 0), (0, LANES - 2 * hb))).reshape(hg, 1, LANES)
            qkv = gdn_preprocess(proj, dn_conv_w[j], qk_dim)
            mixed = gdn_delta_rule(qkv, proj, ba_g, alog_g, dt_g, dn_norm_w[j], n_v_heads, hb=hb)
            w_out = dn_w_out[j]
        else:
            n_heads = fox_b_f.shape[1]
            dim = n_heads * d
            w_in_t = jnp.transpose(fox_w_in[j])
            proj = matmul(xb, w_in_t, 4 * dim)
            w_f = jnp.pad(w_in_t[4 * dim:], ((0, LANES - n_heads), (0, 0)))
            f_logits = matmul(xb, w_f, LANES, tn_pref=LANES)
            c_pad = fox_cumulative_gate(f_logits, _pad_cols(fox_b_f[j][None, :]))
            qa, ka, v_t = fox_preprocess(proj, c_pad, fox_q_norm_w[j], fox_k_norm_w[j], n_heads)
            mixed = fox_attention(qa, ka, v_t, proj, n_heads)
            w_out = fox_w_out[j]
        x2d, xb = matmul_residual_layernorm(mixed, w_out.astype(BF16), x2d, ln1_g[i], ln1_b[i], alpha)
        x2d, xb = _moe_layer(x2d, xb, i, moe_w_rg[i], moe_b_rg[i], moe_w_re[i], moe_b_re[i],
                             moe_w_gate, moe_w_up, moe_w_down, ln2_g[i], ln2_b[i], alpha)
    return x2d.reshape(batch, s, dm)
```
